```python
import jax, jax.numpy as jnp
from jax import lax
import numpy as np

D_MODEL = 1024
BATCH = 2
SEQ = 8192
DEPTH = 1

ATTN_HEADS = 8
HEAD_DIM = 64
ATTN_WIDTH = ATTN_HEADS * HEAD_DIM
GMLP_GROUPS = 8
GMLP_GROUP_DIM = 64
GMLP_WIDTH = GMLP_GROUPS * GMLP_GROUP_DIM
MIX_WIDTH = ATTN_WIDTH + GMLP_WIDTH
IDX_HEADS = 8
IDX_DIM = 64
TOPK_MAX = 256
CHUNK = 128
Q_BLOCK = 128
D_FF = 4 * D_MODEL
ROPE_THETA = 10000.0
EPS = 1e-6
IN_SPLITS = (ATTN_WIDTH, ATTN_WIDTH, ATTN_WIDTH, IDX_HEADS * IDX_DIM, IDX_DIM, IDX_HEADS, GMLP_WIDTH, GMLP_WIDTH)
IN_COLS = sum(IN_SPLITS)

kernel_name = "hymba_gmlp_dsa_hybrid_block"


def rmsnorm(x, g):
    xf = x.astype(jnp.float32)
    y = xf * lax.rsqrt(jnp.mean(xf * xf, axis=-1, keepdims=True) + EPS)
    return (y * g.astype(jnp.float32)).astype(x.dtype)


def rope(x, pos):
    half = x.shape[-1] // 2
    inv_freq = ROPE_THETA ** (-jnp.arange(half, dtype=jnp.float32) / half)
    ang = pos.astype(jnp.float32)[:, None] * inv_freq[None, :]
    cos = jnp.cos(ang)[None, :, None, :]
    sin = jnp.sin(ang)[None, :, None, :]
    xf = x.astype(jnp.float32)
    x1, x2 = xf[..., :half], xf[..., half:]
    out = jnp.concatenate([x1 * cos - x2 * sin, x1 * sin + x2 * cos], axis=-1)
    return out.astype(x.dtype)


def gmlp_mixer(u, v, gm_norm_g, w_spatial, b_spatial):
    B, S, _ = u.shape
    nc = S // CHUNK
    u = jax.nn.gelu(u)
    v = jax.nn.gelu(v).reshape(B, nc, CHUNK, GMLP_GROUPS, GMLP_GROUP_DIM)
    vf = v.astype(jnp.float32)
    mu = jnp.mean(vf, axis=-1, keepdims=True)
    var = jnp.mean(jnp.square(vf - mu), axis=-1, keepdims=True)
    vn = ((vf - mu) * lax.rsqrt(var + EPS)
          * gm_norm_g.astype(jnp.float32).reshape(GMLP_GROUPS, GMLP_GROUP_DIM)).astype(u.dtype)
    causal = jnp.tril(jnp.ones((CHUNK, CHUNK), dtype=bool))
    w = jnp.where(causal[None], w_spatial, jnp.zeros_like(w_spatial))
    mixed = jnp.einsum("gts,bnsgd->bntgd", w, vn) + b_spatial.T[None, None, :, :, None]
    return u * mixed.reshape(B, S, GMLP_WIDTH)


def dsa_mixer(q, k, v, q_idx, k_idx, w_idx):
    B, S = q.shape[0], q.shape[1]
    topk = min(TOPK_MAX, S // 4)
    nb = S // Q_BLOCK
    key_pos = jnp.arange(S)
    attn_scale = HEAD_DIM ** -0.5

    def block(start):
        qb = lax.dynamic_slice_in_dim(q, start, Q_BLOCK, axis=1)
        qib = lax.dynamic_slice_in_dim(q_idx, start, Q_BLOCK, axis=1)
        wb = lax.dynamic_slice_in_dim(w_idx, start, Q_BLOCK, axis=1)
        qpos = start + jnp.arange(Q_BLOCK)
        logits = jnp.einsum("bthd,bsd->bths", qib, k_idx).astype(jnp.float32) * (IDX_DIM ** -0.5)
        iscore = jnp.einsum("bth,bths->bts",
                            wb.astype(jnp.float32) * (IDX_HEADS ** -0.5), jax.nn.relu(logits))
        admissible = key_pos[None, :] <= qpos[:, None]
        iscore = jnp.where(admissible[None], iscore, -jnp.inf)
        _, idx = lax.top_k(iscore, topk)
        kg = jax.vmap(lambda kb, ib: kb[ib])(k, idx)
        vg = jax.vmap(lambda vb, ib: vb[ib])(v, idx)
        s = jnp.einsum("bthd,btkhd->bhtk", qb, kg).astype(jnp.float32) * attn_scale
        valid = idx <= qpos[None, :, None]
        s = jnp.where(valid[:, None], s, -jnp.inf)
        p = jax.nn.softmax(s, axis=-1).astype(v.dtype)
        return jnp.einsum("bhtk,btkhd->bthd", p, vg)

    out = lax.map(block, jnp.arange(nb) * Q_BLOCK)
    return out.transpose(1, 0, 2, 3, 4).reshape(B, S, ATTN_WIDTH)


def setup_inputs(seed: int = 0) -> dict:
    key = jax.random.key(seed)
    ks = jax.random.split(key, 20)
    f32 = jnp.float32
    nrm = lambda k, shape, s: jax.random.normal(k, shape, f32) * s
    return {
        "x": nrm(ks[0], (BATCH, SEQ, D_MODEL), 1.0),
        "c": nrm(ks[1], (BATCH, D_MODEL), 1.0),
        "ada_w": nrm(ks[2], (DEPTH, D_MODEL, 6 * D_MODEL), 0.5 * D_MODEL ** -0.5),
        "ada_b": nrm(ks[3], (DEPTH, 6 * D_MODEL), 0.01),
        "norm1_g": 1.0 + nrm(ks[4], (DEPTH, D_MODEL), 0.02),
        "w_in": nrm(ks[5], (DEPTH, D_MODEL, IN_COLS), D_MODEL ** -0.5),
        "w_spatial": nrm(ks[6], (DEPTH, GMLP_GROUPS, CHUNK, CHUNK), CHUNK ** -0.5),
        "b_spatial": 1.0 + nrm(ks[7], (DEPTH, GMLP_GROUPS, CHUNK), 0.1),
        "gm_norm_g": 1.0 + nrm(ks[8], (DEPTH, GMLP_WIDTH), 0.02),
        "beta_attn": 1.0 + nrm(ks[9], (DEPTH, ATTN_WIDTH), 0.02),
        "beta_gmlp": 1.0 + nrm(ks[10], (DEPTH, GMLP_WIDTH), 0.02),
        "w_out": nrm(ks[11], (DEPTH, MIX_WIDTH, D_MODEL), MIX_WIDTH ** -0.5),
        "norm2_g": 1.0 + nrm(ks[12], (DEPTH, D_MODEL), 0.02),
        "w_ff1": nrm(ks[13], (DEPTH, D_MODEL, D_FF), D_MODEL ** -0.5),
        "w_ff2": nrm(ks[14], (DEPTH, D_FF, D_MODEL), D_FF ** -0.5),
        "final_g": 1.0 + nrm(ks[15], (D_MODEL,), 0.02),
    }


def reference(x, c, ada_w, ada_b, norm1_g, w_in, w_spatial, b_spatial, gm_norm_g,
              beta_attn, beta_gmlp, w_out, norm2_g, w_ff1, w_ff2, final_g):
    B, S, _ = x.shape
    pos = jnp.arange(S)
    split_points = list(np.cumsum(IN_SPLITS)[:-1])
    c_act = jax.nn.silu(c)
    for l in range(DEPTH):
        mod = c_act @ ada_w[l] + ada_b[l]
        sh1, sc1, g1, sh2, sc2, g2 = [m[:, None, :] for m in jnp.split(mod, 6, axis=-1)]

        h = rmsnorm(x, norm1_g[l]) * (1.0 + sc1) + sh1
        proj = h @ w_in[l]
        q, k, v, qi, ki, wi, gu, gv = jnp.split(proj, split_points, axis=-1)
        q = rope(q.reshape(B, S, ATTN_HEADS, HEAD_DIM), pos)
        k = rope(k.reshape(B, S, ATTN_HEADS, HEAD_DIM), pos)
        v = v.reshape(B, S, ATTN_HEADS, HEAD_DIM)
        qi = rope(qi.reshape(B, S, IDX_HEADS, IDX_DIM), pos)
        ki = rope(ki[:, :, None, :], pos)[:, :, 0, :]
        attn_out = dsa_mixer(q, k, v, qi, ki, wi)
        gm_out = gmlp_mixer(gu, gv, gm_norm_g[l], w_spatial[l], b_spatial[l])
        merged = jnp.concatenate([rmsnorm(attn_out, beta_attn[l]),
                                  rmsnorm(gm_out, beta_gmlp[l])], axis=-1)
        x = x + g1 * (merged @ w_out[l])

        h2 = rmsnorm(x, norm2_g[l]) * (1.0 + sc2) + sh2
        ff = jnp.square(jax.nn.relu(h2 @ w_ff1[l])) @ w_ff2[l]
        x = x + g2 * ff
    return rmsnorm(x, final_g)
```

```python
import functools

import jax
import jax.numpy as jnp
from jax import lax
from jax.experimental import pallas as pl
from jax.experimental.pallas import tpu as pltpu

F32 = jnp.float32
BF16 = jnp.bfloat16

LANES = 128
HEAD_DIM = 64
N_HEADS = 8
N_PAIRS = N_HEADS // 2
WIDTH = N_HEADS * HEAD_DIM
CHUNK = 128
TOPK = 256
ROPE_THETA = 10000.0
EPS = 1e-6
LOG2E = 1.4426950408889634
NEG_BIG = -1e30
LOWEST = -3.0e38
VMEM_LIMIT = 56 * 1024 * 1024

TM_IN = 256
TQ = 128
CK_IDX = 256
CK_CNT = 512
CK_ATT = 512
TM_OUT = 512
FF_CHUNK = 1024


def _gelu_tanh(x):
    return 0.5 * x * (1.0 + jnp.tanh(0.7978845608028654 * (x + 0.044715 * x * x * x)))


def _mod_kernel(c_ref, w_ref, b_ref, o_ref):
    c = c_ref[...]
    ca = c / (1.0 + jnp.exp(-c))
    o_ref[...] = jnp.dot(ca, w_ref[...], preferred_element_type=F32,
                         precision=lax.Precision.HIGHEST) + b_ref[...]


def _mod_call(c_pad, w, b):
    rows, d = c_pad.shape
    n = w.shape[1]
    tn = 1024
    return pl.pallas_call(
        _mod_kernel,
        grid=(n // tn,),
        in_specs=[pl.BlockSpec((rows, d), lambda j: (0, 0)),
                  pl.BlockSpec((d, tn), lambda j: (0, j)),
                  pl.BlockSpec((1, tn), lambda j: (0, j))],
        out_specs=pl.BlockSpec((rows, tn), lambda j: (0, j)),
        out_shape=jax.ShapeDtypeStruct((rows, n), F32),
        name="adaln_mod",
    )(c_pad, w, b)


def _inproj_kernel(x_ref, sc_ref, sh_ref, g_ref, wm_ref, wk_ref, cos_ref, sin_ref,
                   wsp_ref, bsp_ref, gmg_ref,
                   q_ref, kT_ref, v_ref, qi_ref, kiT_ref, wi_ref, gm_ref):
    tm = x_ref.shape[1]
    x = x_ref[0]
    ms = jnp.mean(x * x, axis=-1, keepdims=True)
    h = x * lax.rsqrt(ms + EPS) * g_ref[...]
    h = h * (1.0 + sc_ref[0]) + sh_ref[0]
    hb = h.astype(BF16)

    cos = cos_ref[...]
    sin = sin_ref[...]
    lane = lax.broadcasted_iota(jnp.int32, (tm, LANES), 1)
    first_half = (lane & (HEAD_DIM // 2)) == 0
    low_head = lane < HEAD_DIM

    def rope(t):
        partner = jnp.where(first_half, pltpu.roll(t, LANES - HEAD_DIM // 2, 1),
                            pltpu.roll(t, HEAD_DIM // 2, 1))
        return t * cos + partner * sin

    def proj(col, width=WIDTH):
        return jnp.dot(hb, wm_ref[:, col:col + width], preferred_element_type=F32)

    pq = proj(0)
    for j in range(N_PAIRS):
        sl = slice(j * LANES, (j + 1) * LANES)
        q_ref[0, :, sl] = (rope(pq[:, sl]) * (HEAD_DIM ** -0.5 * LOG2E)).astype(BF16)
    pk = proj(WIDTH)
    for j in range(N_PAIRS):
        sl = slice(j * LANES, (j + 1) * LANES)
        kT_ref[0, sl, :] = rope(pk[:, sl]).T.astype(BF16)
    v_ref[0] = proj(2 * WIDTH).astype(BF16)
    pqi = proj(3 * WIDTH)
    for j in range(N_PAIRS):
        sl = slice(j * LANES, (j + 1) * LANES)
        qi_ref[0, :, sl] = (rope(pqi[:, sl]) * (HEAD_DIM ** -0.5)).astype(BF16)

    pkw = jnp.dot(hb, wk_ref[...], preferred_element_type=F32)
    rk = rope(pkw)
    rk2 = jnp.where(low_head, rk, pltpu.roll(rk, HEAD_DIM, 1))
    kiT_ref[0] = rk2.T.astype(BF16)
    wi_ref[0] = pltpu.roll(pkw, HEAD_DIM, 1) * (N_HEADS ** -0.5)

    gu = proj(4 * WIDTH)
    gv = proj(5 * WIDTH)
    lane_c = lax.broadcasted_iota(jnp.int32, (CHUNK, LANES), 1)
    row_c = lax.broadcasted_iota(jnp.int32, (CHUNK, LANES), 0)
    low_c = lane_c < HEAD_DIM
    causal = lane_c <= row_c
    inv_n = 1.0 / HEAD_DIM
    for j in range(N_PAIRS):
        sl = slice(j * LANES, (j + 1) * LANES)
        w_lo = jnp.where(causal, wsp_ref[2 * j], 0.0).astype(BF16)
        w_hi = jnp.where(causal, wsp_ref[2 * j + 1], 0.0).astype(BF16)
        gain = gmg_ref[:, sl]
        bias = bsp_ref[j]
        for cidx in range(tm // CHUNK):
            rs = slice(cidx * CHUNK, (cidx + 1) * CHUNK)
            vv = _gelu_tanh(gv[rs, sl])
            s_all = jnp.sum(vv, axis=-1, keepdims=True)
            s_lo = jnp.sum(jnp.where(low_c, vv, 0.0), axis=-1, keepdims=True)
            mu = jnp.where(low_c, s_lo, s_all - s_lo) * inv_n
            dv = vv - mu
            d2 = dv * dv
            q_all = jnp.sum(d2, axis=-1, keepdims=True)
            q_lo = jnp.sum(jnp.where(low_c, d2, 0.0), axis=-1, keepdims=True)
            var = jnp.where(low_c, q_lo, q_all - q_lo) * inv_n
            vn = (dv * lax.rsqrt(var + EPS) * gain).astype(BF16)
            m_lo = jnp.dot(w_lo, vn, preferred_element_type=F32)
            m_hi = jnp.dot(w_hi, vn, preferred_element_type=F32)
            mixed = jnp.where(low_c, m_lo, m_hi) + bias
            gm_ref[0, rs, sl] = _gelu_tanh(gu[rs, sl]) * mixed


def _inproj_call(x, sc1, sh1, g1n, w_main, w_kw, cos_t, sin_t, wsp, bsp, gmg):
    b, s, d = x.shape
    tm = TM_IN
    const2 = lambda bi, ti: (0, 0)
    const3 = lambda bi, ti: (0, 0, 0)
    tok = lambda bi, ti: (bi, ti, 0)
    return pl.pallas_call(
        _inproj_kernel,
        grid=(b, s // tm),
        in_specs=[pl.BlockSpec((1, tm, d), tok),
                  pl.BlockSpec((1, 1, d), lambda bi, ti: (bi, 0, 0)),
                  pl.BlockSpec((1, 1, d), lambda bi, ti: (bi, 0, 0)),
                  pl.BlockSpec((1, d), const2),
                  pl.BlockSpec(w_main.shape, const2),
                  pl.BlockSpec(w_kw.shape, const2),
                  pl.BlockSpec((tm, LANES), lambda bi, ti: (ti, 0)),
                  pl.BlockSpec((tm, LANES), lambda bi, ti: (ti, 0)),
                  pl.BlockSpec(wsp.shape, const3),
                  pl.BlockSpec(bsp.shape, const3),
                  pl.BlockSpec(gmg.shape, const2)],
        out_specs=[pl.BlockSpec((1, tm, WIDTH), tok),
                   pl.BlockSpec((1, WIDTH, tm), lambda bi, ti: (bi, 0, ti)),
                   pl.BlockSpec((1, tm, WIDTH), tok),
                   pl.BlockSpec((1, tm, WIDTH), tok),
                   pl.BlockSpec((1, LANES, tm), lambda bi, ti: (bi, 0, ti)),
                   pl.BlockSpec((1, tm, LANES), tok),
                   pl.BlockSpec((1, tm, WIDTH), tok)],
        out_shape=[jax.ShapeDtypeStruct((b, s, WIDTH), BF16),
                   jax.ShapeDtypeStruct((b, WIDTH, s), BF16),
                   jax.ShapeDtypeStruct((b, s, WIDTH), BF16),
                   jax.ShapeDtypeStruct((b, s, WIDTH), BF16),
                   jax.ShapeDtypeStruct((b, LANES, s), BF16),
                   jax.ShapeDtypeStruct((b, s, LANES), F32),
                   jax.ShapeDtypeStruct((b, s, WIDTH), F32)],
        compiler_params=pltpu.CompilerParams(
            dimension_semantics=("parallel", "parallel"), vmem_limit_bytes=VMEM_LIMIT),
        name="inproj_gmlp",
    )(x, sc1, sh1, g1n, w_main, w_kw, cos_t, sin_t, wsp, bsp, gmg)


def _attn_kernel(q_ref, qi_ref, wi_ref, kT_ref, v_ref, kiT_ref, o_ref,
                 isc_ref, qm_ref, qim_ref, wb_ref, thr_ref, ubits_ref, done_ref,
                 m_ref, l_ref, acc_ref):
    i = pl.program_id(1)
    q0 = i * TQ
    lane = lax.broadcasted_iota(jnp.int32, (TQ, LANES), 1)
    row = lax.broadcasted_iota(jnp.int32, (TQ, LANES), 0)
    low_head = lane < HEAD_DIM

    qf = q_ref[0].astype(F32)
    qif = qi_ref[0].astype(F32)
    wi = wi_ref[0]
    for h in range(N_HEADS):
        sl = slice((h // 2) * LANES, (h // 2 + 1) * LANES)
        keep = low_head if h % 2 == 0 else jnp.logical_not(low_head)
        qm_ref[h] = jnp.where(keep, qf[:, sl], 0.0).astype(BF16)
        qim_ref[h] = jnp.where(keep, qif[:, sl], 0.0).astype(BF16)
        wb_ref[h] = jnp.broadcast_to(wi[:, h:h + 1], (TQ, LANES))

    n_idx_full = q0 // CK_IDX

    def idx_chunk(c, masked):
        off = pl.multiple_of(c * CK_IDX, CK_IDX)
        kic = kiT_ref[0, :, pl.ds(off, CK_IDX)]
        nb = CK_IDX // LANES
        accs = [None] * nb
        for h in range(N_HEADS):
            lg = jnp.dot(qim_ref[h], kic, preferred_element_type=F32)
            w = wb_ref[h]
            for jb in range(nb):
                t = jnp.maximum(lg[:, jb * LANES:(jb + 1) * LANES], 0.0) * w
                accs[jb] = t if accs[jb] is None else accs[jb] + t
        for jb in range(nb):
            a = accs[jb]
            if masked:
                kidx = off + jb * LANES + lane
                a = jnp.where(kidx <= q0 + row, a, -jnp.inf)
            isc_ref[:, pl.ds(pl.multiple_of(off + jb * LANES, LANES), LANES)] = a

    def idx_body(c, carry):
        idx_chunk(c, False)
        return carry

    lax.fori_loop(0, n_idx_full, idx_body, 0)
    idx_chunk(n_idx_full, True)

    tail_chunks = max(CK_CNT, CK_ATT) // CK_IDX

    @pl.when((n_idx_full + 1) % tail_chunks != 0)
    def _pad_tail():
        off = pl.multiple_of((n_idx_full + 1) * CK_IDX, CK_IDX)
        isc_ref[:, pl.ds(off, CK_IDX)] = jnp.full((TQ, CK_IDX), -jnp.inf, F32)

    n_cnt = q0 // CK_CNT + 1
    needs_search = (q0 + row + 1) > TOPK
    thr_ref[...] = jnp.full((TQ, LANES), LOWEST, F32)

    @pl.when(q0 + TQ > TOPK)
    def _search():
        ubits_ref[...] = jnp.zeros((TQ, LANES), jnp.int32)
        done_ref[...] = jnp.where(needs_search, 0.0, 1.0)
        int_min = jnp.int32(-2 ** 31)

        def to_float(u):
            key = u ^ int_min
            bits = jnp.where(key >= 0, key, key ^ jnp.int32(0x7FFFFFFF))
            return lax.bitcast_convert_type(bits, F32)

        def cond(carry):
            step, active = carry
            return jnp.logical_and(step < 32, active > 0.0)

        def body(carry):
            step, _ = carry
            bit = lax.shift_right_logical(int_min, step)
            cand = ubits_ref[...] | bit
            thr = to_float(cand)

            def cnt_body(c, acc):
                off = pl.multiple_of(c * CK_CNT, CK_CNT)
                for jb in range(CK_CNT // LANES):
                    blk = isc_ref[:, pl.ds(pl.multiple_of(off + jb * LANES, LANES), LANES)]
                    acc = acc + jnp.where(blk >= thr, 1.0, 0.0)
                return acc

            acc = lax.fori_loop(0, n_cnt, cnt_body, jnp.zeros((TQ, LANES), F32))
            cnt = jnp.broadcast_to(jnp.sum(acc, axis=-1, keepdims=True), (TQ, LANES))
            done = done_ref[...]
            take = jnp.logical_and(cnt >= float(TOPK), done == 0.0)
            ubits_ref[...] = jnp.where(take, cand, ubits_ref[...])
            done = jnp.where(jnp.logical_and(take, cnt == float(TOPK)), 1.0, done)
            done_ref[...] = done
            return step + 1, jnp.max(1.0 - done)

        lax.while_loop(cond, body, (jnp.int32(0), jnp.float32(1.0)))
        thr_ref[...] = jnp.where(needs_search, to_float(ubits_ref[...]), LOWEST)

    m_ref[...] = jnp.full(m_ref.shape, NEG_BIG, F32)
    l_ref[...] = jnp.zeros(l_ref.shape, F32)
    acc_ref[...] = jnp.zeros(acc_ref.shape, F32)
    n_att = q0 // CK_ATT + 1
    nb_att = CK_ATT // LANES

    def att_body(c, carry):
        off = pl.multiple_of(c * CK_ATT, CK_ATT)
        thr = thr_ref[...]
        bias = jnp.concatenate(
            [jnp.where(isc_ref[:, pl.ds(pl.multiple_of(off + jb * LANES, LANES), LANES)] >= thr,
                       0.0, NEG_BIG) for jb in range(nb_att)], axis=1)
        for h in range(N_HEADS):
            j = h // 2
            kt = kT_ref[0, j * LANES:(j + 1) * LANES, pl.ds(off, CK_ATT)]
            s = jnp.dot(qm_ref[h], kt, preferred_element_type=F32) + bias
            m_old = m_ref[h]
            m_new = jnp.maximum(m_old, jnp.broadcast_to(
                jnp.max(s, axis=-1, keepdims=True), (TQ, LANES)))
            alpha = jnp.exp2(m_old - m_new)
            p = jnp.exp2(s - jnp.concatenate([m_new] * nb_att, axis=1))
            l_ref[h] = alpha * l_ref[h] + jnp.broadcast_to(
                jnp.sum(p, axis=-1, keepdims=True), (TQ, LANES))
            vv = v_ref[0, pl.ds(off, CK_ATT), j * LANES:(j + 1) * LANES]
            pv = jnp.dot(p.astype(BF16), vv, preferred_element_type=F32)
            acc_ref[h] = alpha * acc_ref[h] + pv
            m_ref[h] = m_new
        return carry

    lax.fori_loop(0, n_att, att_body, 0)

    for j in range(N_PAIRS):
        lo = acc_ref[2 * j] / l_ref[2 * j]
        hi = acc_ref[2 * j + 1] / l_ref[2 * j + 1]
        o_ref[0, :, j * LANES:(j + 1) * LANES] = jnp.where(low_head, lo, hi)


def _attn_call(q, qi, wi, kT, v, kiT):
    b, s, _ = q.shape
    tok = lambda bi, ti: (bi, ti, 0)
    per_batch = lambda bi, ti: (bi, 0, 0)
    resident = functools.partial(pl.BlockSpec, pipeline_mode=pl.Buffered(1))
    return pl.pallas_call(
        _attn_kernel,
        grid=(b, s // TQ),
        in_specs=[pl.BlockSpec((1, TQ, WIDTH), tok),
                  pl.BlockSpec((1, TQ, WIDTH), tok),
                  pl.BlockSpec((1, TQ, LANES), tok),
                  resident((1, WIDTH, s), per_batch),
                  resident((1, s, WIDTH), per_batch),
                  resident((1, LANES, s), per_batch)],
        out_specs=pl.BlockSpec((1, TQ, WIDTH), tok),
        out_shape=jax.ShapeDtypeStruct((b, s, WIDTH), F32),
        scratch_shapes=[pltpu.VMEM((TQ, s), F32),
                        pltpu.VMEM((N_HEADS, TQ, LANES), BF16),
                        pltpu.VMEM((N_HEADS, TQ, LANES), BF16),
                        pltpu.VMEM((N_HEADS, TQ, LANES), F32),
                        pltpu.VMEM((TQ, LANES), F32),
                        pltpu.VMEM((TQ, LANES), jnp.int32),
                        pltpu.VMEM((TQ, LANES), F32),
                        pltpu.VMEM((N_HEADS, TQ, LANES), F32),
                        pltpu.VMEM((N_HEADS, TQ, LANES), F32),
                        pltpu.VMEM((N_HEADS, TQ, LANES), F32)],
        compiler_params=pltpu.CompilerParams(
            dimension_semantics=("parallel", "arbitrary"), vmem_limit_bytes=VMEM_LIMIT),
        name="dsa_attention",
    )(q, qi, wi, kT, v, kiT)


def _out_kernel(x_ref, a_ref, gmix_ref, g1_ref, sc2_ref, sh2_ref, g2_ref,
                ba_ref, bg_ref, n2_ref, fg_ref, wo_ref, w1_ref, w2_ref, o_ref,
                *, apply_final):
    def rms(t, g):
        return t * lax.rsqrt(jnp.mean(t * t, axis=-1, keepdims=True) + EPS) * g

    x = x_ref[0]
    a = rms(a_ref[0], ba_ref[...]).astype(BF16)
    g = rms(gmix_ref[0], bg_ref[...]).astype(BF16)
    y = (jnp.dot(a, wo_ref[0:WIDTH, :], preferred_element_type=F32)
         + jnp.dot(g, wo_ref[WIDTH:2 * WIDTH, :], preferred_element_type=F32))
    x1 = x + g1_ref[0] * y
    h2 = (rms(x1, n2_ref[...]) * (1.0 + sc2_ref[0]) + sh2_ref[0]).astype(BF16)
    ff = None
    d_ff = w1_ref.shape[1]
    for f in range(d_ff // FF_CHUNK):
        fs = slice(f * FF_CHUNK, (f + 1) * FF_CHUNK)
        u = jnp.maximum(jnp.dot(h2, w1_ref[:, fs], preferred_element_type=F32), 0.0)
        part = jnp.dot((u * u).astype(BF16), w2_ref[fs, :], preferred_element_type=F32)
        ff = part if ff is None else ff + part
    x2 = x1 + g2_ref[0] * ff
    o_ref[0] = rms(x2, fg_ref[...]) if apply_final else x2


def _out_call(x, attn, gmix, g1, sc2, sh2, g2, ba, bg, n2, fg, wo, w1, w2, apply_final):
    b, s, d = x.shape
    tm = TM_OUT
    tok = lambda bi, ti: (bi, ti, 0)
    per_b = lambda bi, ti: (bi, 0, 0)
    const2 = lambda bi, ti: (0, 0)
    resident = functools.partial(pl.BlockSpec, pipeline_mode=pl.Buffered(1))
    vec_b = pl.BlockSpec((1, 1, d), per_b)
    return pl.pallas_call(
        functools.partial(_out_kernel, apply_final=apply_final),
        grid=(b, s // tm),
        in_specs=[pl.BlockSpec((1, tm, d), tok),
                  pl.BlockSpec((1, tm, WIDTH), tok),
                  pl.BlockSpec((1, tm, WIDTH), tok),
                  vec_b, vec_b, vec_b, vec_b,
                  pl.BlockSpec((1, WIDTH), const2),
                  pl.BlockSpec((1, WIDTH), const2),
                  pl.BlockSpec((1, d), const2),
                  pl.BlockSpec((1, d), const2),
                  resident(wo.shape, const2),
                  resident(w1.shape, const2),
                  resident(w2.shape, const2)],
        out_specs=pl.BlockSpec((1, tm, d), tok),
        out_shape=jax.ShapeDtypeStruct((b, s, d), F32),
        compiler_params=pltpu.CompilerParams(
            dimension_semantics=("parallel", "parallel"), vmem_limit_bytes=VMEM_LIMIT),
        name="outproj_mlp",
    )(x, attn, gmix, g1, sc2, sh2, g2, ba, bg, n2, fg, wo, w1, w2)


def _rope_tables(s):
    half = HEAD_DIM // 2
    inv_freq = ROPE_THETA ** (-jnp.arange(half, dtype=F32) / half)
    ang = jnp.arange(s).astype(F32)[:, None] * inv_freq[None, :]
    cos = jnp.cos(ang)
    sin = jnp.sin(ang)
    cos_t = jnp.tile(jnp.concatenate([cos, cos], axis=-1), (1, LANES // HEAD_DIM))
    sin_t = jnp.tile(jnp.concatenate([-sin, sin], axis=-1), (1, LANES // HEAD_DIM))
    return cos_t, sin_t


def kernel(x, c, ada_w, ada_b, norm1_g, w_in, w_spatial, b_spatial, gm_norm_g,
           beta_attn, beta_gmlp, w_out, norm2_g, w_ff1, w_ff2, final_g):
    b, s, d = x.shape
    depth = ada_w.shape[0]
    assert d == 2 * WIDTH and w_in.shape[2] == 6 * WIDTH + HEAD_DIM + N_HEADS
    cos_t, sin_t = _rope_tables(s)
    c_pad = jnp.zeros((8, d), F32).at[:b].set(c)
    kw0 = 4 * WIDTH
    kw1 = kw0 + HEAD_DIM + N_HEADS
    for l in range(depth):
        mod = _mod_call(c_pad, ada_w[l], ada_b[l][None, :])[:b]
        sh1, sc1, g1, sh2, sc2, g2 = [m[:, None, :] for m in jnp.split(mod, 6, axis=-1)]
        w_main = jnp.concatenate([w_in[l][:, :kw0], w_in[l][:, kw1:]], axis=1).astype(BF16)
        w_kw = jnp.pad(w_in[l][:, kw0:kw1], ((0, 0), (0, LANES - (kw1 - kw0)))).astype(BF16)
        bsp = jnp.repeat(b_spatial[l].reshape(N_PAIRS, 2, CHUNK).transpose(0, 2, 1),
                         HEAD_DIM, axis=2)
        q, kT, v, qi, kiT, wi, gmix = _inproj_call(
            x, sc1, sh1, norm1_g[l][None, :], w_main, w_kw, cos_t, sin_t,
            w_spatial[l], bsp, gm_norm_g[l][None, :])
        attn = _attn_call(q, qi, wi, kT, v, kiT)
        x = _out_call(x, attn, gmix, g1, sc2, sh2, g2,
                      beta_attn[l][None, :], beta_gmlp[l][None, :], norm2_g[l][None, :],
                      final_g[None, :], w_out[l].astype(BF16), w_ff1[l].astype(BF16),
                      w_ff2[l].astype(BF16), apply_final=(l == depth - 1))
    return x
```

```python
import functools

import jax
import jax.numpy as jnp
from jax import lax
from jax.experimental import pallas as pl
from jax.experimental.pallas import tpu as pltpu

F32 = jnp.float32
BF16 = jnp.bfloat16

LANES = 128
HEAD_DIM = 64
N_HEADS = 8
N_PAIRS = N_HEADS // 2
WIDTH = N_HEADS * HEAD_DIM
CHUNK = 128
TOPK = 256
ROPE_THETA = 10000.0
EPS = 1e-6
LOG2E = 1.4426950408889634
NEG_BIG = -1e30
LOWEST = -3.0e38
VMEM_LIMIT = 56 * 1024 * 1024

TM_IN = 256
TQ = 256
RB_IDX = 128
RB_CNT = 128
CK_IDX = 256
COARSE_BITS = 16
CK_CNT = 512
CK_ATT = 512
TM_OUT = 512
FF_CHUNK = 1024


def _gelu_tanh(x):
    return 0.5 * x * (1.0 + jnp.tanh(0.7978845608028654 * (x + 0.044715 * x * x * x)))


def _mod_kernel(c_ref, w_ref, b_ref, o_ref):
    c = c_ref[...]
    ca = c / (1.0 + jnp.exp(-c))
    o_ref[...] = jnp.dot(ca, w_ref[...], preferred_element_type=F32,
                         precision=lax.Precision.HIGHEST) + b_ref[...]


def _mod_call(c_pad, w, b):
    rows, d = c_pad.shape
    n = w.shape[1]
    tn = 1024
    return pl.pallas_call(
        _mod_kernel,
        grid=(n // tn,),
        in_specs=[pl.BlockSpec((rows, d), lambda j: (0, 0)),
                  pl.BlockSpec((d, tn), lambda j: (0, j)),
                  pl.BlockSpec((1, tn), lambda j: (0, j))],
        out_specs=pl.BlockSpec((rows, tn), lambda j: (0, j)),
        out_shape=jax.ShapeDtypeStruct((rows, n), F32),
        name="adaln_mod",
    )(c_pad, w, b)


def _inproj_kernel(x_ref, sc_ref, sh_ref, g_ref, wm_ref, wk_ref, cos_ref, sin_ref,
                   wsp_ref, bsp_ref, gmg_ref,
                   q_ref, kT_ref, v_ref, qi_ref, kiT_ref, wi_ref, gm_ref):
    tm = x_ref.shape[1]
    x = x_ref[0]
    ms = jnp.mean(x * x, axis=-1, keepdims=True)
    h = x * lax.rsqrt(ms + EPS) * g_ref[...]
    h = h * (1.0 + sc_ref[0]) + sh_ref[0]
    hb = h.astype(BF16)

    cos = cos_ref[...]
    sin = sin_ref[...]
    lane = lax.broadcasted_iota(jnp.int32, (tm, LANES), 1)
    first_half = (lane & (HEAD_DIM // 2)) == 0
    low_head = lane < HEAD_DIM

    def rope(t):
        partner = jnp.where(first_half, pltpu.roll(t, LANES - HEAD_DIM // 2, 1),
                            pltpu.roll(t, HEAD_DIM // 2, 1))
        return t * cos + partner * sin

    def proj(col, width=WIDTH):
        return jnp.dot(hb, wm_ref[:, col:col + width], preferred_element_type=F32)

    pq = proj(0)
    for j in range(N_PAIRS):
        sl = slice(j * LANES, (j + 1) * LANES)
        q_ref[0, :, sl] = (rope(pq[:, sl]) * (HEAD_DIM ** -0.5 * LOG2E)).astype(BF16)
    pk = proj(WIDTH)
    for j in range(N_PAIRS):
        sl = slice(j * LANES, (j + 1) * LANES)
        kT_ref[0, sl, :] = rope(pk[:, sl]).T.astype(BF16)
    v_ref[0] = proj(2 * WIDTH).astype(BF16)
    pqi = proj(3 * WIDTH)
    for j in range(N_PAIRS):
        sl = slice(j * LANES, (j + 1) * LANES)
        qi_ref[0, :, sl] = (rope(pqi[:, sl]) * (HEAD_DIM ** -0.5)).astype(BF16)

    pkw = jnp.dot(hb, wk_ref[...], preferred_element_type=F32)
    rk = rope(pkw)
    rk2 = jnp.where(low_head, rk, pltpu.roll(rk, HEAD_DIM, 1))
    kiT_ref[0] = rk2.T.astype(BF16)
    wi_ref[0] = pltpu.roll(pkw, HEAD_DIM, 1) * (N_HEADS ** -0.5)

    gu = proj(4 * WIDTH)
    gv = proj(5 * WIDTH)
    lane_c = lax.broadcasted_iota(jnp.int32, (CHUNK, LANES), 1)
    row_c = lax.broadcasted_iota(jnp.int32, (CHUNK, LANES), 0)
    low_c = lane_c < HEAD_DIM
    causal = lane_c <= row_c
    inv_n = 1.0 / HEAD_DIM
    for j in range(N_PAIRS):
        sl = slice(j * LANES, (j + 1) * LANES)
        w_lo = jnp.where(causal, wsp_ref[2 * j], 0.0).astype(BF16)
        w_hi = jnp.where(causal, wsp_ref[2 * j + 1], 0.0).astype(BF16)
        gain = gmg_ref[:, sl]
        bias = bsp_ref[j]
        for cidx in range(tm // CHUNK):
            rs = slice(cidx * CHUNK, (cidx + 1) * CHUNK)
            vv = _gelu_tanh(gv[rs, sl])
            s_all = jnp.sum(vv, axis=-1, keepdims=True)
            s_lo = jnp.sum(jnp.where(low_c, vv, 0.0), axis=-1, keepdims=True)
            mu = jnp.where(low_c, s_lo, s_all - s_lo) * inv_n
            dv = vv - mu
            d2 = dv * dv
            q_all = jnp.sum(d2, axis=-1, keepdims=True)
            q_lo = jnp.sum(jnp.where(low_c, d2, 0.0), axis=-1, keepdims=True)
            var = jnp.where(low_c, q_lo, q_all - q_lo) * inv_n
            vn = (dv * lax.rsqrt(var + EPS) * gain).astype(BF16)
            m_lo = jnp.dot(w_lo, vn, preferred_element_type=F32)
            m_hi = jnp.dot(w_hi, vn, preferred_element_type=F32)
            mixed = jnp.where(low_c, m_lo, m_hi) + bias
            gm_ref[0, rs, sl] = _gelu_tanh(gu[rs, sl]) * mixed


def _inproj_call(x, sc1, sh1, g1n, w_main, w_kw, cos_t, sin_t, wsp, bsp, gmg):
    b, s, d = x.shape
    tm = TM_IN
    const2 = lambda bi, ti: (0, 0)
    const3 = lambda bi, ti: (0, 0, 0)
    tok = lambda bi, ti: (bi, ti, 0)
    return pl.pallas_call(
        _inproj_kernel,
        grid=(b, s // tm),
        in_specs=[pl.BlockSpec((1, tm, d), tok),
                  pl.BlockSpec((1, 1, d), lambda bi, ti: (bi, 0, 0)),
                  pl.BlockSpec((1, 1, d), lambda bi, ti: (bi, 0, 0)),
                  pl.BlockSpec((1, d), const2),
                  pl.BlockSpec(w_main.shape, const2),
                  pl.BlockSpec(w_kw.shape, const2),
                  pl.BlockSpec((tm, LANES), lambda bi, ti: (ti, 0)),
                  pl.BlockSpec((tm, LANES), lambda bi, ti: (ti, 0)),
                  pl.BlockSpec(wsp.shape, const3),
                  pl.BlockSpec(bsp.shape, const3),
                  pl.BlockSpec(gmg.shape, const2)],
        out_specs=[pl.BlockSpec((1, tm, WIDTH), tok),
                   pl.BlockSpec((1, WIDTH, tm), lambda bi, ti: (bi, 0, ti)),
                   pl.BlockSpec((1, tm, WIDTH), tok),
                   pl.BlockSpec((1, tm, WIDTH), tok),
                   pl.BlockSpec((1, LANES, tm), lambda bi, ti: (bi, 0, ti)),
                   pl.BlockSpec((1, tm, LANES), tok),
                   pl.BlockSpec((1, tm, WIDTH), tok)],
        out_shape=[jax.ShapeDtypeStruct((b, s, WIDTH), BF16),
                   jax.ShapeDtypeStruct((b, WIDTH, s), BF16),
                   jax.ShapeDtypeStruct((b, s, WIDTH), BF16),
                   jax.ShapeDtypeStruct((b, s, WIDTH), BF16),
                   jax.ShapeDtypeStruct((b, LANES, s), BF16),
                   jax.ShapeDtypeStruct((b, s, LANES), F32),
                   jax.ShapeDtypeStruct((b, s, WIDTH), F32)],
        compiler_params=pltpu.CompilerParams(
            dimension_semantics=("parallel", "parallel"), vmem_limit_bytes=VMEM_LIMIT),
        name="inproj_gmlp",
    )(x, sc1, sh1, g1n, w_main, w_kw, cos_t, sin_t, wsp, bsp, gmg)


def _attn_kernel(q_ref, qi_ref, wi_ref, kT_ref, v_ref, kiT_ref, o_ref,
                 isc_ref, isc16_ref, qm_ref, qim_ref, wb_ref, thr_ref, ubits_ref, done_ref,
                 bias_ref, s_ref, p_ref, alpha_ref, m_ref, l_ref, acc_ref):
    i = pl.program_id(1)
    q0 = i * TQ
    lane = lax.broadcasted_iota(jnp.int32, (TQ, LANES), 1)
    row = lax.broadcasted_iota(jnp.int32, (TQ, LANES), 0)
    low_head = lane < HEAD_DIM

    qf = q_ref[0].astype(F32)
    qif = qi_ref[0].astype(F32)
    wi = wi_ref[0]
    for h in range(N_HEADS):
        sl = slice((h // 2) * LANES, (h // 2 + 1) * LANES)
        keep = low_head if h % 2 == 0 else jnp.logical_not(low_head)
        qm_ref[h] = jnp.where(keep, qf[:, sl], 0.0).astype(BF16)
        qim_ref[h] = jnp.where(keep, qif[:, sl], 0.0).astype(BF16)
        wb_ref[h] = jnp.broadcast_to(wi[:, h:h + 1], (TQ, LANES))

    n_idx_full = q0 // CK_IDX

    high_bits = jnp.int32(-(1 << (32 - COARSE_BITS)))
    lane_rb = lax.broadcasted_iota(jnp.int32, (RB_IDX, LANES), 1)
    row_rb = lax.broadcasted_iota(jnp.int32, (RB_IDX, LANES), 0)

    def truncate(t):
        bits = lax.bitcast_convert_type(t, jnp.int32) & high_bits
        return lax.bitcast_convert_type(bits, F32).astype(BF16)

    def idx_chunk(c, masked):
        off = pl.multiple_of(c * CK_IDX, CK_IDX)
        kic = kiT_ref[0, :, pl.ds(off, CK_IDX)]
        nb = CK_IDX // LANES
        for r in range(TQ // RB_IDX):
            rs = slice(r * RB_IDX, (r + 1) * RB_IDX)
            accs = [None] * nb
            for h in range(N_HEADS):
                lg = jnp.dot(qim_ref[h, rs, :], kic, preferred_element_type=F32)
                w = wb_ref[h, rs, :]
                for jb in range(nb):
                    t = jnp.maximum(lg[:, jb * LANES:(jb + 1) * LANES], 0.0) * w
                    accs[jb] = t if accs[jb] is None else accs[jb] + t
            for jb in range(nb):
                a = accs[jb]
                if masked:
                    kidx = off + jb * LANES + lane_rb
                    a = jnp.where(kidx <= q0 + r * RB_IDX + row_rb, a, -jnp.inf)
                col = pl.ds(pl.multiple_of(off + jb * LANES, LANES), LANES)
                isc_ref[rs, col] = a
                isc16_ref[rs, col] = truncate(a)

    def idx_body(c, carry):
        idx_chunk(c, False)
        return carry

    lax.fori_loop(0, n_idx_full, idx_body, 0)
    idx_chunk(n_idx_full, True)

    tail_chunks = max(CK_CNT, CK_ATT) // CK_IDX
    assert tail_chunks in (1, 2) and CK_IDX % TQ == 0

    @pl.when((n_idx_full + 1) % tail_chunks != 0)
    def _pad_tail():
        off = pl.multiple_of((n_idx_full + 1) * CK_IDX, CK_IDX)
        isc_ref[:, pl.ds(off, CK_IDX)] = jnp.full((TQ, CK_IDX), -jnp.inf, F32)
        isc16_ref[:, pl.ds(off, CK_IDX)] = jnp.full((TQ, CK_IDX), -jnp.inf, BF16)

    n_cnt = q0 // CK_CNT + 1
    needs_search = (q0 + row + 1) > TOPK
    thr_ref[...] = jnp.full((TQ, LANES), LOWEST, F32)

    @pl.when(q0 + TQ > TOPK)
    def _search():
        ubits_ref[...] = jnp.zeros((TQ, LANES), jnp.int32)
        done_ref[...] = jnp.where(needs_search, 0.0, 1.0)
        int_min = jnp.int32(-2 ** 31)

        def to_float(u):
            key = u ^ int_min
            bits = jnp.where(key >= 0, key, key ^ jnp.int32(0x7FFFFFFF))
            return lax.bitcast_convert_type(bits, F32)

        def search_step(step, coarse):
            src_ref, dtype = (isc16_ref, BF16) if coarse else (isc_ref, F32)
            one = jnp.ones((), dtype)
            zero = jnp.zeros((), dtype)
            bit = lax.shift_right_logical(int_min, step)
            open_rows = []
            for r in range(TQ // RB_CNT):
                rs = slice(r * RB_CNT, (r + 1) * RB_CNT)
                cand = ubits_ref[rs, :] | bit
                thr = to_float(cand)
                if coarse:
                    thr = truncate(thr)

                def cnt_body(c, acc, rs=rs, thr=thr):
                    off = pl.multiple_of(c * CK_CNT, CK_CNT)
                    for jb in range(CK_CNT // LANES):
                        col = pl.ds(pl.multiple_of(off + jb * LANES, LANES), LANES)
                        acc = acc + jnp.where(src_ref[rs, col] >= thr, one, zero)
                    return acc

                acc = lax.fori_loop(0, n_cnt, cnt_body, jnp.zeros((RB_CNT, LANES), dtype))
                cnt = jnp.broadcast_to(
                    jnp.sum(acc.astype(F32), axis=-1, keepdims=True), (RB_CNT, LANES))
                done = done_ref[rs, :]
                take = jnp.logical_and(cnt >= float(TOPK), done == 0.0)
                ubits_ref[rs, :] = jnp.where(take, cand, ubits_ref[rs, :])
                done = jnp.where(jnp.logical_and(take, cnt == float(TOPK)), 1.0, done)
                done_ref[rs, :] = done
                open_rows.append(1.0 - done)
            return functools.reduce(jnp.maximum, open_rows)

        def coarse_body(step, carry):
            search_step(step, True)
            return carry

        lax.fori_loop(0, COARSE_BITS, coarse_body, 0)

        def cond(carry):
            step, active = carry
            return jnp.logical_and(step < 32, active > 0.0)

        def fine_body(carry):
            step, _ = carry
            return step + 1, jnp.max(search_step(step, False))

        lax.while_loop(cond, fine_body,
                       (jnp.int32(COARSE_BITS), jnp.max(1.0 - done_ref[...])))
        thr_ref[...] = jnp.where(needs_search, to_float(ubits_ref[...]), LOWEST)

        @pl.when(jnp.max(1.0 - done_ref[...]) > 0.0)
        def _ties():
            thr = thr_ref[...]
            open_row = done_ref[...] == 0.0

            def gt_body(c, acc):
                off = pl.multiple_of(c * CK_CNT, CK_CNT)
                for jb in range(CK_CNT // LANES):
                    blk = isc_ref[:, pl.ds(pl.multiple_of(off + jb * LANES, LANES), LANES)]
                    acc = acc + jnp.where(blk > thr, 1.0, 0.0)
                return acc

            gt = lax.fori_loop(0, n_cnt, gt_body, jnp.zeros((TQ, LANES), F32))
            gt = jnp.broadcast_to(jnp.sum(gt, axis=-1, keepdims=True), (TQ, LANES))
            quota = jnp.where(open_row, float(TOPK) - gt, 3.0e38)
            sq_r = lax.broadcasted_iota(jnp.int32, (LANES, LANES), 0)
            sq_c = lax.broadcasted_iota(jnp.int32, (LANES, LANES), 1)
            prefix = jnp.where(sq_r <= sq_c, 1.0, 0.0).astype(BF16)

            def tie_body(c, seen):
                col = pl.ds(pl.multiple_of(c * LANES, LANES), LANES)
                blk = isc_ref[:, col]
                tied = blk == thr
                rank = seen + jnp.dot(jnp.where(tied, 1.0, 0.0).astype(BF16), prefix,
                                      preferred_element_type=F32)
                drop = jnp.logical_and(tied, rank > quota)
                isc_ref[:, col] = jnp.where(drop, -jnp.inf, blk)
                return jnp.broadcast_to(jnp.max(rank, axis=-1, keepdims=True), (TQ, LANES))

            lax.fori_loop(0, n_cnt * (CK_CNT // LANES), tie_body,
                          jnp.zeros((TQ, LANES), F32))

    m_ref[...] = jnp.full(m_ref.shape, NEG_BIG, F32)
    l_ref[...] = jnp.zeros(l_ref.shape, F32)
    acc_ref[...] = jnp.zeros(acc_ref.shape, F32)
    n_att = q0 // CK_ATT + 1
    nb_att = CK_ATT // LANES

    def att_body(c, carry):
        off = pl.multiple_of(c * CK_ATT, CK_ATT)
        thr = thr_ref[...]
        for jb in range(nb_att):
            col = pl.ds(pl.multiple_of(off + jb * LANES, LANES), LANES)
            bias_ref[:, jb * LANES:(jb + 1) * LANES] = jnp.where(
                isc_ref[:, col] >= thr, 0.0, NEG_BIG)
        for h in range(N_HEADS):
            j = h // 2
            kt = kT_ref[0, j * LANES:(j + 1) * LANES, pl.ds(off, CK_ATT)]
            s = jnp.dot(qm_ref[h], kt, preferred_element_type=F32) + bias_ref[...]
            s_ref[h] = s
            mx = s[:, 0:LANES]
            for jb in range(1, nb_att):
                mx = jnp.maximum(mx, s[:, jb * LANES:(jb + 1) * LANES])
            m_old = m_ref[h]
            m_new = jnp.maximum(m_old, jnp.broadcast_to(
                jnp.max(mx, axis=-1, keepdims=True), (TQ, LANES)))
            alpha_ref[h] = jnp.exp2(m_old - m_new)
            m_ref[h] = m_new
        for h in range(N_HEADS):
            m_new = m_ref[h]
            psum = None
            for jb in range(nb_att):
                sl = slice(jb * LANES, (jb + 1) * LANES)
                p = jnp.exp2(s_ref[h, :, sl] - m_new)
                psum = p if psum is None else psum + p
                p_ref[h, :, sl] = p.astype(BF16)
            l_ref[h] = alpha_ref[h] * l_ref[h] + psum
        for h in range(N_HEADS):
            j = h // 2
            vv = v_ref[0, pl.ds(off, CK_ATT), j * LANES:(j + 1) * LANES]
            pv = jnp.dot(p_ref[h], vv, preferred_element_type=F32)
            acc_ref[h] = alpha_ref[h] * acc_ref[h] + pv
        return carry

    lax.fori_loop(0, n_att, att_body, 0)

    def normalized(h):
        return acc_ref[h] / jnp.sum(l_ref[h], axis=-1, keepdims=True)

    for j in range(N_PAIRS):
        o_ref[0, :, j * LANES:(j + 1) * LANES] = jnp.where(
            low_head, normalized(2 * j), normalized(2 * j + 1))


def _attn_call(q, qi, wi, kT, v, kiT):
    b, s, _ = q.shape
    tok = lambda bi, ti: (bi, ti, 0)
    per_batch = lambda bi, ti: (bi, 0, 0)
    resident = functools.partial(pl.BlockSpec, pipeline_mode=pl.Buffered(1))
    return pl.pallas_call(
        _attn_kernel,
        grid=(b, s // TQ),
        in_specs=[pl.BlockSpec((1, TQ, WIDTH), tok),
                  pl.BlockSpec((1, TQ, WIDTH), tok),
                  pl.BlockSpec((1, TQ, LANES), tok),
                  resident((1, WIDTH, s), per_batch),
                  resident((1, s, WIDTH), per_batch),
                  resident((1, LANES, s), per_batch)],
        out_specs=pl.BlockSpec((1, TQ, WIDTH), tok),
        out_shape=jax.ShapeDtypeStruct((b, s, WIDTH), F32),
        scratch_shapes=[pltpu.VMEM((TQ, s), F32),
                        pltpu.VMEM((TQ, s), BF16),
                        pltpu.VMEM((N_HEADS, TQ, LANES), BF16),
                        pltpu.VMEM((N_HEADS, TQ, LANES), BF16),
                        pltpu.VMEM((N_HEADS, TQ, LANES), F32),
                        pltpu.VMEM((TQ, LANES), F32),
                        pltpu.VMEM((TQ, LANES), jnp.int32),
                        pltpu.VMEM((TQ, LANES), F32),
                        pltpu.VMEM((TQ, CK_ATT), F32),
                        pltpu.VMEM((N_HEADS, TQ, CK_ATT), F32),
                        pltpu.VMEM((N_HEADS, TQ, CK_ATT), BF16),
                        pltpu.VMEM((N_HEADS, TQ, LANES), F32),
                        pltpu.VMEM((N_HEADS, TQ, LANES), F32),
                        pltpu.VMEM((N_HEADS, TQ, LANES), F32),
                        pltpu.VMEM((N_HEADS, TQ, LANES), F32)],
        compiler_params=pltpu.CompilerParams(
            dimension_semantics=("parallel", "arbitrary"), vmem_limit_bytes=VMEM_LIMIT),
        name="dsa_attention",
    )(q, qi, wi, kT, v, kiT)


def _out_kernel(x_ref, a_ref, gmix_ref, g1_ref, sc2_ref, sh2_ref, g2_ref,
                ba_ref, bg_ref, n2_ref, fg_ref, wo_ref, w1_ref, w2_ref, o_ref,
                *, apply_final):
    def rms(t, g):
        return t * lax.rsqrt(jnp.mean(t * t, axis=-1, keepdims=True) + EPS) * g

    x = x_ref[0]
    a = rms(a_ref[0], ba_ref[...]).astype(BF16)
    g = rms(gmix_ref[0], bg_ref[...]).astype(BF16)
    y = (jnp.dot(a, wo_ref[0:WIDTH, :], preferred_element_type=F32)
         + jnp.dot(g, wo_ref[WIDTH:2 * WIDTH, :], preferred_element_type=F32))
    x1 = x + g1_ref[0] * y
    h2 = (rms(x1, n2_ref[...]) * (1.0 + sc2_ref[0]) + sh2_ref[0]).astype(BF16)
    ff = None
    d_ff = w1_ref.shape[1]
    for f in range(d_ff // FF_CHUNK):
        fs = slice(f * FF_CHUNK, (f + 1) * FF_CHUNK)
        u = jnp.maximum(jnp.dot(h2, w1_ref[:, fs], preferred_element_type=F32), 0.0)
        part = jnp.dot((u * u).astype(BF16), w2_ref[fs, :], preferred_element_type=F32)
        ff = part if ff is None else ff + part
    x2 = x1 + g2_ref[0] * ff
    o_ref[0] = rms(x2, fg_ref[...]) if apply_final else x2


def _out_call(x, attn, gmix, g1, sc2, sh2, g2, ba, bg, n2, fg, wo, w1, w2, apply_final):
    b, s, d = x.shape
    tm = TM_OUT
    tok = lambda bi, ti: (bi, ti, 0)
    per_b = lambda bi, ti: (bi, 0, 0)
    const2 = lambda bi, ti: (0, 0)
    resident = functools.partial(pl.BlockSpec, pipeline_mode=pl.Buffered(1))
    vec_b = pl.BlockSpec((1, 1, d), per_b)
    return pl.pallas_call(
        functools.partial(_out_kernel, apply_final=apply_final),
        grid=(b, s // tm),
        in_specs=[pl.BlockSpec((1, tm, d), tok),
                  pl.BlockSpec((1, tm, WIDTH), tok),
                  pl.BlockSpec((1, tm, WIDTH), tok),
                  vec_b, vec_b, vec_b, vec_b,
                  pl.BlockSpec((1, WIDTH), const2),
                  pl.BlockSpec((1, WIDTH), const2),
                  pl.BlockSpec((1, d), const2),
                  pl.BlockSpec((1, d), const2),
                  resident(wo.shape, const2),
                  resident(w1.shape, const2),
                  resident(w2.shape, const2)],
        out_specs=pl.BlockSpec((1, tm, d), tok),
        out_shape=jax.ShapeDtypeStruct((b, s, d), F32),
        compiler_params=pltpu.CompilerParams(
            dimension_semantics=("parallel", "parallel"), vmem_limit_bytes=VMEM_LIMIT),
        name="outproj_mlp",
    )(x, attn, gmix, g1, sc2, sh2, g2, ba, bg, n2, fg, wo, w1, w2)


def _rope_tables(s):
    half = HEAD_DIM // 2
    inv_freq = ROPE_THETA ** (-jnp.arange(half, dtype=F32) / half)
    ang = jnp.arange(s).astype(F32)[:, None] * inv_freq[None, :]
    cos = jnp.cos(ang)
    sin = jnp.sin(ang)
    cos_t = jnp.tile(jnp.concatenate([cos, cos], axis=-1), (1, LANES // HEAD_DIM))
    sin_t = jnp.tile(jnp.concatenate([-sin, sin], axis=-1), (1, LANES // HEAD_DIM))
    return cos_t, sin_t


def kernel(x, c, ada_w, ada_b, norm1_g, w_in, w_spatial, b_spatial, gm_norm_g,
           beta_attn, beta_gmlp, w_out, norm2_g, w_ff1, w_ff2, final_g):
    b, s, d = x.shape
    depth = ada_w.shape[0]
    assert d == 2 * WIDTH and w_in.shape[2] == 6 * WIDTH + HEAD_DIM + N_HEADS
    cos_t, sin_t = _rope_tables(s)
    c_pad = jnp.zeros((8, d), F32).at[:b].set(c)
    kw0 = 4 * WIDTH
    kw1 = kw0 + HEAD_DIM + N_HEADS
    for l in range(depth):
        mod = _mod_call(c_pad, ada_w[l], ada_b[l][None, :])[:b]
        sh1, sc1, g1, sh2, sc2, g2 = [m[:, None, :] for m in jnp.split(mod, 6, axis=-1)]
        w_main = jnp.concatenate([w_in[l][:, :kw0], w_in[l][:, kw1:]], axis=1).astype(BF16)
        w_kw = jnp.pad(w_in[l][:, kw0:kw1], ((0, 0), (0, LANES - (kw1 - kw0)))).astype(BF16)
        bsp = jnp.repeat(b_spatial[l].reshape(N_PAIRS, 2, CHUNK).transpose(0, 2, 1),
                         HEAD_DIM, axis=2)
        q, kT, v, qi, kiT, wi, gmix = _inproj_call(
            x, sc1, sh1, norm1_g[l][None, :], w_main, w_kw, cos_t, sin_t,
            w_spatial[l], bsp, gm_norm_g[l][None, :])
        attn = _attn_call(q, qi, wi, kT, v, kiT)
        x = _out_call(x, attn, gmix, g1, sc2, sh2, g2,
                      beta_attn[l][None, :], beta_gmlp[l][None, :], norm2_g[l][None, :],
                      final_g[None, :], w_out[l].astype(BF16), w_ff1[l].astype(BF16),
                      w_ff2[l].astype(BF16), apply_final=(l == depth - 1))
    return x
```

```python
import functools

import jax
import jax.numpy as jnp
from jax import lax
from jax.experimental import pallas as pl
from jax.experimental.pallas import tpu as pltpu

F32 = jnp.float32
BF16 = jnp.bfloat16

LANES = 128
HEAD_DIM = 64
N_HEADS = 8
N_PAIRS = N_HEADS // 2
WIDTH = N_HEADS * HEAD_DIM
CHUNK = 128
TOPK = 256
ROPE_THETA = 10000.0
EPS = 1e-6
LOG2E = 1.4426950408889634
NEG_BIG = -1e30
LOWEST = -3.0e38
VMEM_LIMIT = 56 * 1024 * 1024

TM_IN = 256
TQ = 256
RB_IDX = 128
RB_CNT = 128
CK_IDX = 256
COARSE_BITS = 16
CK_CNT = 512
CK_ATT = 512
TM_OUT = 512
FF_CHUNK = 1024


def _gelu_tanh(x):
    return 0.5 * x * (1.0 + jnp.tanh(0.7978845608028654 * (x + 0.044715 * x * x * x)))


def _mod_kernel(c_ref, w_ref, b_ref, o_ref):
    c = c_ref[...]
    ca = c / (1.0 + jnp.exp(-c))
    w = w_ref[...]
    ca_hi = ca.astype(BF16)
    ca_lo = (ca - ca_hi.astype(F32)).astype(BF16)
    w_hi = w.astype(BF16)
    w_lo = (w - w_hi.astype(F32)).astype(BF16)
    dot = functools.partial(jnp.dot, preferred_element_type=F32)
    o_ref[...] = dot(ca_hi, w_hi) + (dot(ca_hi, w_lo) + dot(ca_lo, w_hi)) + b_ref[...]


def _mod_call(c_pad, w, b):
    rows, d = c_pad.shape
    n = w.shape[1]
    tn = 1024
    return pl.pallas_call(
        _mod_kernel,
        grid=(n // tn,),
        in_specs=[pl.BlockSpec((rows, d), lambda j: (0, 0)),
                  pl.BlockSpec((d, tn), lambda j: (0, j)),
                  pl.BlockSpec((1, tn), lambda j: (0, j))],
        out_specs=pl.BlockSpec((rows, tn), lambda j: (0, j)),
        out_shape=jax.ShapeDtypeStruct((rows, n), F32),
        name="adaln_mod",
    )(c_pad, w, b)


def _inproj_kernel(x_ref, sc_ref, sh_ref, g_ref, wm_ref, wk_ref, cos_ref, sin_ref,
                   wsp_ref, bsp_ref, gmg_ref,
                   q_ref, kT_ref, v_ref, qi_ref, kiT_ref, wi_ref, gm_ref):
    tm = x_ref.shape[1]
    x = x_ref[0]
    ms = jnp.mean(x * x, axis=-1, keepdims=True)
    h = x * lax.rsqrt(ms + EPS) * g_ref[...]
    h = h * (1.0 + sc_ref[0]) + sh_ref[0]
    hb = h.astype(BF16)

    cos = cos_ref[...]
    sin = sin_ref[...]
    lane = lax.broadcasted_iota(jnp.int32, (tm, LANES), 1)
    first_half = (lane & (HEAD_DIM // 2)) == 0
    low_head = lane < HEAD_DIM

    def rope(t):
        partner = jnp.where(first_half, pltpu.roll(t, LANES - HEAD_DIM // 2, 1),
                            pltpu.roll(t, HEAD_DIM // 2, 1))
        return t * cos + partner * sin

    def proj(col, width=WIDTH):
        return jnp.dot(hb, wm_ref[:, col:col + width], preferred_element_type=F32)

    pq = proj(0)
    for j in range(N_PAIRS):
        sl = slice(j * LANES, (j + 1) * LANES)
        q_ref[0, :, sl] = (rope(pq[:, sl]) * (HEAD_DIM ** -0.5 * LOG2E)).astype(BF16)
    pk = proj(WIDTH)
    for j in range(N_PAIRS):
        sl = slice(j * LANES, (j + 1) * LANES)
        kT_ref[0, sl, :] = rope(pk[:, sl]).T.astype(BF16)
    v_ref[0] = proj(2 * WIDTH).astype(BF16)
    pqi = proj(3 * WIDTH)
    for j in range(N_PAIRS):
        sl = slice(j * LANES, (j + 1) * LANES)
        qi_ref[0, :, sl] = (rope(pqi[:, sl]) * (HEAD_DIM ** -0.5)).astype(BF16)

    pkw = jnp.dot(hb, wk_ref[...], preferred_element_type=F32)
    rk = rope(pkw)
    rk2 = jnp.where(low_head, rk, pltpu.roll(rk, HEAD_DIM, 1))
    kiT_ref[0] = rk2.T.astype(BF16)
    wi_ref[0] = pltpu.roll(pkw, HEAD_DIM, 1) * (N_HEADS ** -0.5)

    gu = proj(4 * WIDTH)
    gv = proj(5 * WIDTH)
    lane_c = lax.broadcasted_iota(jnp.int32, (CHUNK, LANES), 1)
    row_c = lax.broadcasted_iota(jnp.int32, (CHUNK, LANES), 0)
    low_c = lane_c < HEAD_DIM
    causal = lane_c <= row_c
    inv_n = 1.0 / HEAD_DIM
    for j in range(N_PAIRS):
        sl = slice(j * LANES, (j + 1) * LANES)
        w_lo = jnp.where(causal, wsp_ref[2 * j], 0.0).astype(BF16)
        w_hi = jnp.where(causal, wsp_ref[2 * j + 1], 0.0).astype(BF16)
        gain = gmg_ref[:, sl]
        bias = bsp_ref[j]
        for cidx in range(tm // CHUNK):
            rs = slice(cidx * CHUNK, (cidx + 1) * CHUNK)
            vv = _gelu_tanh(gv[rs, sl])
            s_all = jnp.sum(vv, axis=-1, keepdims=True)
            s_lo = jnp.sum(jnp.where(low_c, vv, 0.0), axis=-1, keepdims=True)
            mu = jnp.where(low_c, s_lo, s_all - s_lo) * inv_n
            dv = vv - mu
            d2 = dv * dv
            q_all = jnp.sum(d2, axis=-1, keepdims=True)
            q_lo = jnp.sum(jnp.where(low_c, d2, 0.0), axis=-1, keepdims=True)
            var = jnp.where(low_c, q_lo, q_all - q_lo) * inv_n
            vn = (dv * lax.rsqrt(var + EPS) * gain).astype(BF16)
            m_lo = jnp.dot(w_lo, vn, preferred_element_type=F32)
            m_hi = jnp.dot(w_hi, vn, preferred_element_type=F32)
            mixed = jnp.where(low_c, m_lo, m_hi) + bias
            gm_ref[0, rs, sl] = _gelu_tanh(gu[rs, sl]) * mixed


def _inproj_call(x, sc1, sh1, g1n, w_main, w_kw, cos_t, sin_t, wsp, bsp, gmg):
    b, s, d = x.shape
    tm = TM_IN
    const2 = lambda bi, ti: (0, 0)
    const3 = lambda bi, ti: (0, 0, 0)
    tok = lambda bi, ti: (bi, ti, 0)
    return pl.pallas_call(
        _inproj_kernel,
        grid=(b, s // tm),
        in_specs=[pl.BlockSpec((1, tm, d), tok),
                  pl.BlockSpec((1, 1, d), lambda bi, ti: (bi, 0, 0)),
                  pl.BlockSpec((1, 1, d), lambda bi, ti: (bi, 0, 0)),
                  pl.BlockSpec((1, d), const2),
                  pl.BlockSpec(w_main.shape, const2),
                  pl.BlockSpec(w_kw.shape, const2),
                  pl.BlockSpec((tm, LANES), lambda bi, ti: (ti, 0)),
                  pl.BlockSpec((tm, LANES), lambda bi, ti: (ti, 0)),
                  pl.BlockSpec(wsp.shape, const3),
                  pl.BlockSpec(bsp.shape, const3),
                  pl.BlockSpec(gmg.shape, const2)],
        out_specs=[pl.BlockSpec((1, tm, WIDTH), tok),
                   pl.BlockSpec((1, WIDTH, tm), lambda bi, ti: (bi, 0, ti)),
                   pl.BlockSpec((1, tm, WIDTH), tok),
                   pl.BlockSpec((1, tm, WIDTH), tok),
                   pl.BlockSpec((1, LANES, tm), lambda bi, ti: (bi, 0, ti)),
                   pl.BlockSpec((1, tm, LANES), tok),
                   pl.BlockSpec((1, tm, WIDTH), tok)],
        out_shape=[jax.ShapeDtypeStruct((b, s, WIDTH), BF16),
                   jax.ShapeDtypeStruct((b, WIDTH, s), BF16),
                   jax.ShapeDtypeStruct((b, s, WIDTH), BF16),
                   jax.ShapeDtypeStruct((b, s, WIDTH), BF16),
                   jax.ShapeDtypeStruct((b, LANES, s), BF16),
                   jax.ShapeDtypeStruct((b, s, LANES), F32),
                   jax.ShapeDtypeStruct((b, s, WIDTH), F32)],
        compiler_params=pltpu.CompilerParams(
            dimension_semantics=("parallel", "parallel"), vmem_limit_bytes=VMEM_LIMIT),
        name="inproj_gmlp",
    )(x, sc1, sh1, g1n, w_main, w_kw, cos_t, sin_t, wsp, bsp, gmg)


def _attn_kernel(q_ref, qi_ref, wi_ref, kT_ref, v_ref, kiT_ref, o_ref,
                 isc_ref, isc16_ref, qm_ref, qim_ref, wb_ref, cpos_ref, cnn_ref,
                 thr_ref, ubits_ref, width_ref, done_ref, tie_ref,
                 bias_ref, s_ref, p_ref, alpha_ref, m_ref, l_ref, acc_ref):
    i = pl.program_id(1)
    q0 = i * TQ
    lane = lax.broadcasted_iota(jnp.int32, (TQ, LANES), 1)
    row = lax.broadcasted_iota(jnp.int32, (TQ, LANES), 0)
    low_head = lane < HEAD_DIM

    qf = q_ref[0].astype(F32)
    qif = qi_ref[0].astype(F32)
    wi = wi_ref[0]
    for h in range(N_HEADS):
        sl = slice((h // 2) * LANES, (h // 2 + 1) * LANES)
        keep = low_head if h % 2 == 0 else jnp.logical_not(low_head)
        qm_ref[h] = jnp.where(keep, qf[:, sl], 0.0).astype(BF16)
        qim_ref[h] = jnp.where(keep, qif[:, sl], 0.0).astype(BF16)
        wb_ref[h] = jnp.broadcast_to(wi[:, h:h + 1], (TQ, LANES))

    n_idx_full = q0 // CK_IDX
    lane_rb = lax.broadcasted_iota(jnp.int32, (RB_IDX, LANES), 1)
    row_rb = lax.broadcasted_iota(jnp.int32, (RB_IDX, LANES), 0)
    cpos_ref[...] = jnp.zeros((TQ, LANES), F32)
    cnn_ref[...] = jnp.zeros((TQ, LANES), F32)

    def idx_chunk(c, masked):
        off = pl.multiple_of(c * CK_IDX, CK_IDX)
        kic = kiT_ref[0, :, pl.ds(off, CK_IDX)]
        nb = CK_IDX // LANES
        for r in range(TQ // RB_IDX):
            rs = slice(r * RB_IDX, (r + 1) * RB_IDX)
            accs = [None] * nb
            for h in range(N_HEADS):
                lg = jnp.dot(qim_ref[h, rs, :], kic, preferred_element_type=F32)
                w = wb_ref[h, rs, :]
                for jb in range(nb):
                    t = jnp.maximum(lg[:, jb * LANES:(jb + 1) * LANES], 0.0) * w
                    accs[jb] = t if accs[jb] is None else accs[jb] + t
            cpos = cpos_ref[rs, :]
            cnn = cnn_ref[rs, :]
            for jb in range(nb):
                a = accs[jb]
                if masked:
                    kidx = off + jb * LANES + lane_rb
                    a = jnp.where(kidx <= q0 + r * RB_IDX + row_rb, a, -jnp.inf)
                col = pl.ds(pl.multiple_of(off + jb * LANES, LANES), LANES)
                isc_ref[rs, col] = a
                isc16_ref[rs, col] = a.astype(BF16)
                cpos = cpos + jnp.where(a > 0.0, 1.0, 0.0)
                cnn = cnn + jnp.where(a >= 0.0, 1.0, 0.0)
            cpos_ref[rs, :] = cpos
            cnn_ref[rs, :] = cnn

    def idx_body(c, carry):
        idx_chunk(c, False)
        return carry

    lax.fori_loop(0, n_idx_full, idx_body, 0)
    idx_chunk(n_idx_full, True)

    tail_chunks = max(CK_CNT, CK_ATT) // CK_IDX
    assert tail_chunks in (1, 2) and CK_IDX % TQ == 0

    @pl.when((n_idx_full + 1) % tail_chunks != 0)
    def _pad_tail():
        off = pl.multiple_of((n_idx_full + 1) * CK_IDX, CK_IDX)
        isc_ref[:, pl.ds(off, CK_IDX)] = jnp.full((TQ, CK_IDX), -jnp.inf, F32)
        isc16_ref[:, pl.ds(off, CK_IDX)] = jnp.full((TQ, CK_IDX), -jnp.inf, BF16)

    n_cnt = q0 // CK_CNT + 1
    needs_search = (q0 + row + 1) > TOPK
    thr_ref[...] = jnp.full((TQ, LANES), LOWEST, F32)

    @pl.when(q0 + TQ > TOPK)
    def _search():
        int_min = jnp.int32(-2 ** 31)
        flip = jnp.int32(0x7FFFFFFF)
        high_bits = jnp.int32(-(1 << (32 - COARSE_BITS)))
        half_bucket = 1 << (31 - COARSE_BITS)

        def to_float(u):
            key = u ^ int_min
            return lax.bitcast_convert_type(jnp.where(key >= 0, key, key ^ flip), F32)

        def from_float(f):
            bits = lax.bitcast_convert_type(f, jnp.int32)
            return jnp.where(bits >= 0, bits, bits ^ flip) ^ int_min

        def bucket_value(u):
            bits = lax.bitcast_convert_type(to_float(u), jnp.int32) & high_bits
            return lax.bitcast_convert_type(bits, F32)

        def row_total(ref):
            return jnp.broadcast_to(jnp.sum(ref[...], axis=-1, keepdims=True), (TQ, LANES))

        def count_ge(src_ref, dtype, rs, thr):
            one = jnp.ones((), dtype)
            zero = jnp.zeros((), dtype)

            def cnt_body(c, acc):
                off = pl.multiple_of(c * CK_CNT, CK_CNT)
                for jb in range(CK_CNT // LANES):
                    col = pl.ds(pl.multiple_of(off + jb * LANES, LANES), LANES)
                    acc = acc + jnp.where(src_ref[rs, col] >= thr, one, zero)
                return acc

            acc = lax.fori_loop(0, n_cnt, cnt_body, jnp.zeros((RB_CNT, LANES), dtype))
            return jnp.broadcast_to(
                jnp.sum(acc.astype(F32), axis=-1, keepdims=True), (RB_CNT, LANES))

        zero_tie = jnp.logical_and(
            needs_search, jnp.logical_and(row_total(cpos_ref) < float(TOPK),
                                          row_total(cnn_ref) >= float(TOPK)))
        ubits_ref[...] = jnp.zeros((TQ, LANES), jnp.int32)
        tie_ref[...] = jnp.where(zero_tie, 1.0, 0.0)
        thr_ref[...] = jnp.where(zero_tie, 0.0, LOWEST)
        done_ref[...] = jnp.where(
            jnp.logical_and(needs_search, jnp.logical_not(zero_tie)), 0.0, 1.0)

        def coarse_body(step, carry):
            bit = lax.shift_right_logical(int_min, step)
            for r in range(TQ // RB_CNT):
                rs = slice(r * RB_CNT, (r + 1) * RB_CNT)
                cand = ubits_ref[rs, :] | bit
                cnt = count_ge(isc16_ref, BF16, rs, bucket_value(cand).astype(BF16))
                ubits_ref[rs, :] = jnp.where(cnt >= float(TOPK), cand, ubits_ref[rs, :])
            return carry

        lax.fori_loop(0, COARSE_BITS, coarse_body, 0)

        ubits_ref[...] = from_float(bucket_value(ubits_ref[...])) - half_bucket
        width_ref[...] = jnp.full((TQ, LANES), 3 * half_bucket, jnp.int32)

        def cond(carry):
            step, active = carry
            return jnp.logical_and(step < COARSE_BITS + 4, active > 0.0)

        def fine_body(carry):
            step, _ = carry
            open_rows = []
            for r in range(TQ // RB_CNT):
                rs = slice(r * RB_CNT, (r + 1) * RB_CNT)
                lo = ubits_ref[rs, :]
                width = width_ref[rs, :]
                done = done_ref[rs, :]
                half = lax.shift_right_logical(width, 1)
                mid = lo + half
                thr = to_float(mid)
                cnt = count_ge(isc_ref, F32, rs, thr)
                active = done == 0.0
                ge = cnt >= float(TOPK)
                new_lo = jnp.where(ge, mid, lo)
                new_width = jnp.where(ge, width - half, half)
                hit = jnp.logical_and(active, cnt == float(TOPK))
                closed = jnp.logical_and(
                    active, jnp.logical_and(jnp.logical_not(hit), new_width <= 1))
                thr_ref[rs, :] = jnp.where(
                    hit, thr, jnp.where(closed, to_float(new_lo), thr_ref[rs, :]))
                tie_ref[rs, :] = jnp.where(closed, 1.0, tie_ref[rs, :])
                ubits_ref[rs, :] = jnp.where(active, new_lo, lo)
                width_ref[rs, :] = jnp.where(active, new_width, width)
                done = jnp.where(jnp.logical_or(hit, closed), 1.0, done)
                done_ref[rs, :] = done
                open_rows.append(1.0 - done)
            return step + 1, jnp.max(functools.reduce(jnp.maximum, open_rows))

        lax.while_loop(cond, fine_body, (jnp.int32(0), jnp.max(1.0 - done_ref[...])))

        @pl.when(jnp.max(tie_ref[...]) > 0.0)
        def _ties():
            thr = thr_ref[...]

            def gt_body(c, acc):
                off = pl.multiple_of(c * CK_CNT, CK_CNT)
                for jb in range(CK_CNT // LANES):
                    blk = isc_ref[:, pl.ds(pl.multiple_of(off + jb * LANES, LANES), LANES)]
                    acc = acc + jnp.where(blk > thr, 1.0, 0.0)
                return acc

            gt = lax.fori_loop(0, n_cnt, gt_body, jnp.zeros((TQ, LANES), F32))
            gt = jnp.broadcast_to(jnp.sum(gt, axis=-1, keepdims=True), (TQ, LANES))
            quota = jnp.where(tie_ref[...] > 0.0, float(TOPK) - gt, 3.0e38)
            sq_r = lax.broadcasted_iota(jnp.int32, (LANES, LANES), 0)
            sq_c = lax.broadcasted_iota(jnp.int32, (LANES, LANES), 1)
            prefix = jnp.where(sq_r <= sq_c, 1.0, 0.0).astype(BF16)
            total = jnp.ones((LANES, LANES), BF16)

            def tie_body(c, seen):
                off = pl.multiple_of(c * CK_CNT, CK_CNT)
                for jb in range(CK_CNT // LANES):
                    col = pl.ds(pl.multiple_of(off + jb * LANES, LANES), LANES)
                    blk = isc_ref[:, col]
                    tied = blk == thr
                    tied_b = jnp.where(tied, 1.0, 0.0).astype(BF16)
                    rank = seen + jnp.dot(tied_b, prefix, preferred_element_type=F32)
                    drop = jnp.logical_and(tied, rank > quota)
                    isc_ref[:, col] = jnp.where(drop, -jnp.inf, blk)
                    seen = seen + jnp.dot(tied_b, total, preferred_element_type=F32)
                return seen

            lax.fori_loop(0, n_cnt, tie_body, jnp.zeros((TQ, LANES), F32))

    m_ref[...] = jnp.full(m_ref.shape, NEG_BIG, F32)
    l_ref[...] = jnp.zeros(l_ref.shape, F32)
    acc_ref[...] = jnp.zeros(acc_ref.shape, F32)
    n_att = q0 // CK_ATT + 1
    nb_att = CK_ATT // LANES

    def att_body(c, carry):
        off = pl.multiple_of(c * CK_ATT, CK_ATT)
        thr = thr_ref[...]
        for jb in range(nb_att):
            col = pl.ds(pl.multiple_of(off + jb * LANES, LANES), LANES)
            bias_ref[:, jb * LANES:(jb + 1) * LANES] = jnp.where(
                isc_ref[:, col] >= thr, 0.0, NEG_BIG)
        for h in range(N_HEADS):
            j = h // 2
            kt = kT_ref[0, j * LANES:(j + 1) * LANES, pl.ds(off, CK_ATT)]
            s = jnp.dot(qm_ref[h], kt, preferred_element_type=F32) + bias_ref[...]
            s_ref[h] = s
            mx = s[:, 0:LANES]
            for jb in range(1, nb_att):
                mx = jnp.maximum(mx, s[:, jb * LANES:(jb + 1) * LANES])
            m_old = m_ref[h]
            m_new = jnp.maximum(m_old, jnp.broadcast_to(
                jnp.max(mx, axis=-1, keepdims=True), (TQ, LANES)))
            alpha_ref[h] = jnp.exp2(m_old - m_new)
            m_ref[h] = m_new
        for h in range(N_HEADS):
            m_new = m_ref[h]
            psum = None
            for jb in range(nb_att):
                sl = slice(jb * LANES, (jb + 1) * LANES)
                p = jnp.exp2(s_ref[h, :, sl] - m_new)
                psum = p if psum is None else psum + p
                p_ref[h, :, sl] = p.astype(BF16)
            l_ref[h] = alpha_ref[h] * l_ref[h] + psum
        for h in range(N_HEADS):
            j = h // 2
            vv = v_ref[0, pl.ds(off, CK_ATT), j * LANES:(j + 1) * LANES]
            pv = jnp.dot(p_ref[h], vv, preferred_element_type=F32)
            acc_ref[h] = alpha_ref[h] * acc_ref[h] + pv
        return carry

    lax.fori_loop(0, n_att, att_body, 0)

    def normalized(h):
        return acc_ref[h] / jnp.sum(l_ref[h], axis=-1, keepdims=True)

    for j in range(N_PAIRS):
        o_ref[0, :, j * LANES:(j + 1) * LANES] = jnp.where(
            low_head, normalized(2 * j), normalized(2 * j + 1))


def _attn_call(q, qi, wi, kT, v, kiT):
    b, s, _ = q.shape
    tok = lambda bi, ti: (bi, ti, 0)
    per_batch = lambda bi, ti: (bi, 0, 0)
    resident = functools.partial(pl.BlockSpec, pipeline_mode=pl.Buffered(1))
    return pl.pallas_call(
        _attn_kernel,
        grid=(b, s // TQ),
        in_specs=[pl.BlockSpec((1, TQ, WIDTH), tok),
                  pl.BlockSpec((1, TQ, WIDTH), tok),
                  pl.BlockSpec((1, TQ, LANES), tok),
                  resident((1, WIDTH, s), per_batch),
                  resident((1, s, WIDTH), per_batch),
                  resident((1, LANES, s), per_batch)],
        out_specs=pl.BlockSpec((1, TQ, WIDTH), tok),
        out_shape=jax.ShapeDtypeStruct((b, s, WIDTH), F32),
        scratch_shapes=[pltpu.VMEM((TQ, s), F32),
                        pltpu.VMEM((TQ, s), BF16),
                        pltpu.VMEM((N_HEADS, TQ, LANES), BF16),
                        pltpu.VMEM((N_HEADS, TQ, LANES), BF16),
                        pltpu.VMEM((N_HEADS, TQ, LANES), F32),
                        pltpu.VMEM((TQ, LANES), F32),
                        pltpu.VMEM((TQ, LANES), F32),
                        pltpu.VMEM((TQ, LANES), F32),
                        pltpu.VMEM((TQ, LANES), jnp.int32),
                        pltpu.VMEM((TQ, LANES), jnp.int32),
                        pltpu.VMEM((TQ, LANES), F32),
                        pltpu.VMEM((TQ, LANES), F32),
                        pltpu.VMEM((TQ, CK_ATT), F32),
                        pltpu.VMEM((N_HEADS, TQ, CK_ATT), F32),
                        pltpu.VMEM((N_HEADS, TQ, CK_ATT), BF16),
                        pltpu.VMEM((N_HEADS, TQ, LANES), F32),
                        pltpu.VMEM((N_HEADS, TQ, LANES), F32),
                        pltpu.VMEM((N_HEADS, TQ, LANES), F32),
                        pltpu.VMEM((N_HEADS, TQ, LANES), F32)],
        compiler_params=pltpu.CompilerParams(
            dimension_semantics=("parallel", "arbitrary"), vmem_limit_bytes=VMEM_LIMIT),
        name="dsa_attention",
    )(q, qi, wi, kT, v, kiT)


def _out_kernel(x_ref, a_ref, gmix_ref, g1_ref, sc2_ref, sh2_ref, g2_ref,
                ba_ref, bg_ref, n2_ref, fg_ref, wo_ref, w1_ref, w2_ref, o_ref,
                *, apply_final):
    def rms(t, g):
        return t * lax.rsqrt(jnp.mean(t * t, axis=-1, keepdims=True) + EPS) * g

    x = x_ref[0]
    a = rms(a_ref[0], ba_ref[...]).astype(BF16)
    g = rms(gmix_ref[0], bg_ref[...]).astype(BF16)
    y = (jnp.dot(a, wo_ref[0:WIDTH, :], preferred_element_type=F32)
         + jnp.dot(g, wo_ref[WIDTH:2 * WIDTH, :], preferred_element_type=F32))
    x1 = x + g1_ref[0] * y
    h2 = (rms(x1, n2_ref[...]) * (1.0 + sc2_ref[0]) + sh2_ref[0]).astype(BF16)
    ff = None
    d_ff = w1_ref.shape[1]
    for f in range(d_ff // FF_CHUNK):
        fs = slice(f * FF_CHUNK, (f + 1) * FF_CHUNK)
        u = jnp.maximum(jnp.dot(h2, w1_ref[:, fs], preferred_element_type=F32), 0.0)
        part = jnp.dot((u * u).astype(BF16), w2_ref[fs, :], preferred_element_type=F32)
        ff = part if ff is None else ff + part
    x2 = x1 + g2_ref[0] * ff
    o_ref[0] = rms(x2, fg_ref[...]) if apply_final else x2


def _out_call(x, attn, gmix, g1, sc2, sh2, g2, ba, bg, n2, fg, wo, w1, w2, apply_final):
    b, s, d = x.shape
    tm = TM_OUT
    tok = lambda bi, ti: (bi, ti, 0)
    per_b = lambda bi, ti: (bi, 0, 0)
    const2 = lambda bi, ti: (0, 0)
    resident = functools.partial(pl.BlockSpec, pipeline_mode=pl.Buffered(1))
    vec_b = pl.BlockSpec((1, 1, d), per_b)
    return pl.pallas_call(
        functools.partial(_out_kernel, apply_final=apply_final),
        grid=(b, s // tm),
        in_specs=[pl.BlockSpec((1, tm, d), tok),
                  pl.BlockSpec((1, tm, WIDTH), tok),
                  pl.BlockSpec((1, tm, WIDTH), tok),
                  vec_b, vec_b, vec_b, vec_b,
                  pl.BlockSpec((1, WIDTH), const2),
                  pl.BlockSpec((1, WIDTH), const2),
                  pl.BlockSpec((1, d), const2),
                  pl.BlockSpec((1, d), const2),
                  resident(wo.shape, const2),
                  resident(w1.shape, const2),
                  resident(w2.shape, const2)],
        out_specs=pl.BlockSpec((1, tm, d), tok),
        out_shape=jax.ShapeDtypeStruct((b, s, d), F32),
        compiler_params=pltpu.CompilerParams(
            dimension_semantics=("parallel", "parallel"), vmem_limit_bytes=VMEM_LIMIT),
        name="outproj_mlp",
    )(x, attn, gmix, g1, sc2, sh2, g2, ba, bg, n2, fg, wo, w1, w2)


def _rope_tables(s):
    half = HEAD_DIM // 2
    inv_freq = ROPE_THETA ** (-jnp.arange(half, dtype=F32) / half)
    ang = jnp.arange(s).astype(F32)[:, None] * inv_freq[None, :]
    cos = jnp.cos(ang)
    sin = jnp.sin(ang)
    cos_t = jnp.tile(jnp.concatenate([cos, cos], axis=-1), (1, LANES // HEAD_DIM))
    sin_t = jnp.tile(jnp.concatenate([-sin, sin], axis=-1), (1, LANES // HEAD_DIM))
    return cos_t, sin_t


def kernel(x, c, ada_w, ada_b, norm1_g, w_in, w_spatial, b_spatial, gm_norm_g,
           beta_attn, beta_gmlp, w_out, norm2_g, w_ff1, w_ff2, final_g):
    b, s, d = x.shape
    depth = ada_w.shape[0]
    assert d == 2 * WIDTH and w_in.shape[2] == 6 * WIDTH + HEAD_DIM + N_HEADS
    cos_t, sin_t = _rope_tables(s)
    c_pad = jnp.zeros((8, d), F32).at[:b].set(c)
    kw0 = 4 * WIDTH
    kw1 = kw0 + HEAD_DIM + N_HEADS
    for l in range(depth):
        mod = _mod_call(c_pad, ada_w[l], ada_b[l][None, :])[:b]
        sh1, sc1, g1, sh2, sc2, g2 = [m[:, None, :] for m in jnp.split(mod, 6, axis=-1)]
        w_main = jnp.concatenate([w_in[l][:, :kw0], w_in[l][:, kw1:]], axis=1).astype(BF16)
        w_kw = jnp.pad(w_in[l][:, kw0:kw1], ((0, 0), (0, LANES - (kw1 - kw0)))).astype(BF16)
        bsp = jnp.repeat(b_spatial[l].reshape(N_PAIRS, 2, CHUNK).transpose(0, 2, 1),
                         HEAD_DIM, axis=2)
        q, kT, v, qi, kiT, wi, gmix = _inproj_call(
            x, sc1, sh1, norm1_g[l][None, :], w_main, w_kw, cos_t, sin_t,
            w_spatial[l], bsp, gm_norm_g[l][None, :])
        attn = _attn_call(q, qi, wi, kT, v, kiT)
        x = _out_call(x, attn, gmix, g1, sc2, sh2, g2,
                      beta_attn[l][None, :], beta_gmlp[l][None, :], norm2_g[l][None, :],
                      final_g[None, :], w_out[l].astype(BF16), w_ff1[l].astype(BF16),
                      w_ff2[l].astype(BF16), apply_final=(l == depth - 1))
    return x
```

```python
import functools

import jax
import jax.numpy as jnp
from jax import lax
from jax.experimental import pallas as pl
from jax.experimental.pallas import tpu as pltpu

F32 = jnp.float32
BF16 = jnp.bfloat16

LANES = 128
HEAD_DIM = 64
N_HEADS = 8
N_PAIRS = N_HEADS // 2
WIDTH = N_HEADS * HEAD_DIM
CHUNK = 128
TOPK = 256
ROPE_THETA = 10000.0
EPS = 1e-6
LOG2E = 1.4426950408889634
NEG_BIG = -1e30
LOWEST = -3.0e38
VMEM_LIMIT = 56 * 1024 * 1024

TM_IN = 256
TQ = 256
SUBLANES = 8
KB_IDX = 128
CK_IDX = 256
KB_TIE = 256
COARSE_BITS = 16
CK_CNT = 512
KB_CNT = 128
CK_ATT = 512
KB_ATT = 128
TM_OUT = 512
FF_CHUNK = 1024


def _gelu_tanh(x):
    return 0.5 * x * (1.0 + jnp.tanh(0.7978845608028654 * (x + 0.044715 * x * x * x)))


def _mod_kernel(c_ref, w_ref, b_ref, o_ref):
    c = c_ref[...]
    ca = c / (1.0 + jnp.exp(-c))
    w = w_ref[...]
    ca_hi = ca.astype(BF16)
    ca_lo = (ca - ca_hi.astype(F32)).astype(BF16)
    w_hi = w.astype(BF16)
    w_lo = (w - w_hi.astype(F32)).astype(BF16)
    dot = functools.partial(jnp.dot, preferred_element_type=F32)
    o_ref[...] = dot(ca_hi, w_hi) + (dot(ca_hi, w_lo) + dot(ca_lo, w_hi)) + b_ref[...]


def _mod_call(c_pad, w, b):
    rows, d = c_pad.shape
    n = w.shape[1]
    tn = 1024
    return pl.pallas_call(
        _mod_kernel,
        grid=(n // tn,),
        in_specs=[pl.BlockSpec((rows, d), lambda j: (0, 0)),
                  pl.BlockSpec((d, tn), lambda j: (0, j)),
                  pl.BlockSpec((1, tn), lambda j: (0, j))],
        out_specs=pl.BlockSpec((rows, tn), lambda j: (0, j)),
        out_shape=jax.ShapeDtypeStruct((rows, n), F32),
        name="adaln_mod",
    )(c_pad, w, b)


def _inproj_kernel(x_ref, sc_ref, sh_ref, g_ref, wm_ref, wk_ref, cos_ref, sin_ref,
                   wsp_ref, bsp_ref, gmg_ref,
                   qT_ref, k_ref, vT_ref, qiT_ref, ki_ref, wT_ref, gm_ref):
    tm = x_ref.shape[1]
    x = x_ref[0]
    ms = jnp.mean(x * x, axis=-1, keepdims=True)
    h = x * lax.rsqrt(ms + EPS) * g_ref[...]
    h = h * (1.0 + sc_ref[0]) + sh_ref[0]
    hb = h.astype(BF16)

    cos = cos_ref[...]
    sin = sin_ref[...]
    lane = lax.broadcasted_iota(jnp.int32, (tm, LANES), 1)
    first_half = (lane & (HEAD_DIM // 2)) == 0
    low_head = lane < HEAD_DIM

    def rope(t):
        partner = jnp.where(first_half, pltpu.roll(t, LANES - HEAD_DIM // 2, 1),
                            pltpu.roll(t, HEAD_DIM // 2, 1))
        return t * cos + partner * sin

    def proj(col, width=WIDTH):
        return jnp.dot(hb, wm_ref[:, col:col + width], preferred_element_type=F32)

    pq = proj(0)
    for j in range(N_PAIRS):
        sl = slice(j * LANES, (j + 1) * LANES)
        qT_ref[0, sl, :] = (rope(pq[:, sl]) * (HEAD_DIM ** -0.5 * LOG2E)).T.astype(BF16)
    pk = proj(WIDTH)
    for j in range(N_PAIRS):
        sl = slice(j * LANES, (j + 1) * LANES)
        k_ref[0, :, sl] = rope(pk[:, sl]).astype(BF16)
    pv = proj(2 * WIDTH)
    for j in range(N_PAIRS):
        sl = slice(j * LANES, (j + 1) * LANES)
        vT_ref[0, sl, :] = pv[:, sl].T.astype(BF16)
    pqi = proj(3 * WIDTH)
    for j in range(N_PAIRS):
        sl = slice(j * LANES, (j + 1) * LANES)
        qiT_ref[0, sl, :] = (rope(pqi[:, sl]) * (HEAD_DIM ** -0.5)).T.astype(BF16)

    pkw = jnp.dot(hb, wk_ref[...], preferred_element_type=F32)
    rk = rope(pkw)
    ki_ref[0] = jnp.where(low_head, rk, pltpu.roll(rk, HEAD_DIM, 1)).astype(BF16)
    wT_ref[0] = pkw.T[HEAD_DIM:HEAD_DIM + N_HEADS, :] * (N_HEADS ** -0.5)

    gu = proj(4 * WIDTH)
    gv = proj(5 * WIDTH)
    lane_c = lax.broadcasted_iota(jnp.int32, (CHUNK, LANES), 1)
    row_c = lax.broadcasted_iota(jnp.int32, (CHUNK, LANES), 0)
    low_c = lane_c < HEAD_DIM
    causal = lane_c <= row_c
    inv_n = 1.0 / HEAD_DIM
    for j in range(N_PAIRS):
        sl = slice(j * LANES, (j + 1) * LANES)
        w_lo = jnp.where(causal, wsp_ref[2 * j], 0.0).astype(BF16)
        w_hi = jnp.where(causal, wsp_ref[2 * j + 1], 0.0).astype(BF16)
        gain = gmg_ref[:, sl]
        bias = bsp_ref[j]
        for cidx in range(tm // CHUNK):
            rs = slice(cidx * CHUNK, (cidx + 1) * CHUNK)
            vv = _gelu_tanh(gv[rs, sl])
            s_all = jnp.sum(vv, axis=-1, keepdims=True)
            s_lo = jnp.sum(jnp.where(low_c, vv, 0.0), axis=-1, keepdims=True)
            mu = jnp.where(low_c, s_lo, s_all - s_lo) * inv_n
            dv = vv - mu
            d2 = dv * dv
            q_all = jnp.sum(d2, axis=-1, keepdims=True)
            q_lo = jnp.sum(jnp.where(low_c, d2, 0.0), axis=-1, keepdims=True)
            var = jnp.where(low_c, q_lo, q_all - q_lo) * inv_n
            vn = (dv * lax.rsqrt(var + EPS) * gain).astype(BF16)
            m_lo = jnp.dot(w_lo, vn, preferred_element_type=F32)
            m_hi = jnp.dot(w_hi, vn, preferred_element_type=F32)
            mixed = jnp.where(low_c, m_lo, m_hi) + bias
            gm_ref[0, rs, sl] = _gelu_tanh(gu[rs, sl]) * mixed


def _inproj_call(x, sc1, sh1, g1n, w_main, w_kw, cos_t, sin_t, wsp, bsp, gmg):
    b, s, d = x.shape
    tm = TM_IN
    const2 = lambda bi, ti: (0, 0)
    const3 = lambda bi, ti: (0, 0, 0)
    tok = lambda bi, ti: (bi, ti, 0)
    tok_t = lambda bi, ti: (bi, 0, ti)
    return pl.pallas_call(
        _inproj_kernel,
        grid=(b, s // tm),
        in_specs=[pl.BlockSpec((1, tm, d), tok),
                  pl.BlockSpec((1, 1, d), lambda bi, ti: (bi, 0, 0)),
                  pl.BlockSpec((1, 1, d), lambda bi, ti: (bi, 0, 0)),
                  pl.BlockSpec((1, d), const2),
                  pl.BlockSpec(w_main.shape, const2),
                  pl.BlockSpec(w_kw.shape, const2),
                  pl.BlockSpec((tm, LANES), lambda bi, ti: (ti, 0)),
                  pl.BlockSpec((tm, LANES), lambda bi, ti: (ti, 0)),
                  pl.BlockSpec(wsp.shape, const3),
                  pl.BlockSpec(bsp.shape, const3),
                  pl.BlockSpec(gmg.shape, const2)],
        out_specs=[pl.BlockSpec((1, WIDTH, tm), tok_t),
                   pl.BlockSpec((1, tm, WIDTH), tok),
                   pl.BlockSpec((1, WIDTH, tm), tok_t),
                   pl.BlockSpec((1, WIDTH, tm), tok_t),
                   pl.BlockSpec((1, tm, LANES), tok),
                   pl.BlockSpec((1, N_HEADS, tm), tok_t),
                   pl.BlockSpec((1, tm, WIDTH), tok)],
        out_shape=[jax.ShapeDtypeStruct((b, WIDTH, s), BF16),
                   jax.ShapeDtypeStruct((b, s, WIDTH), BF16),
                   jax.ShapeDtypeStruct((b, WIDTH, s), BF16),
                   jax.ShapeDtypeStruct((b, WIDTH, s), BF16),
                   jax.ShapeDtypeStruct((b, s, LANES), BF16),
                   jax.ShapeDtypeStruct((b, N_HEADS, s), F32),
                   jax.ShapeDtypeStruct((b, s, WIDTH), F32)],
        compiler_params=pltpu.CompilerParams(
            dimension_semantics=("parallel", "parallel"), vmem_limit_bytes=VMEM_LIMIT),
        name="inproj_gmlp",
    )(x, sc1, sh1, g1n, w_main, w_kw, cos_t, sin_t, wsp, bsp, gmg)


def _tiles(x, rows=SUBLANES):
    return x.reshape(x.shape[0] // rows, rows, x.shape[1])


def _fold(x3, op, chains=4):
    accs = [x3[i] for i in range(min(chains, x3.shape[0]))]
    for i in range(len(accs), x3.shape[0]):
        accs[i % chains] = op(accs[i % chains], x3[i])
    while len(accs) > 1:
        accs = [op(accs[i], accs[i + 1]) if i + 1 < len(accs) else accs[i]
                for i in range(0, len(accs), 2)]
    return accs[0]


def _attn_kernel(qT_ref, qiT_ref, wT_ref, k_ref, vT_ref, ki_ref, o_ref,
                 isc_ref, isc16_ref, qm_ref, qim_ref, cpos_ref, cnn_ref,
                 thr_ref, lo_ref, width_ref, done_ref, tie_ref,
                 bias_ref, s_ref, p_ref, alpha_ref, m_ref, l_ref, acc_ref):
    i = pl.program_id(1)
    q0 = i * TQ
    state = (SUBLANES, TQ)
    top_half = lax.broadcasted_iota(jnp.int32, (LANES, TQ), 0) < HEAD_DIM

    def replicate(row):
        return jnp.broadcast_to(row, state)

    qT = qT_ref[0].astype(F32)
    qiT = qiT_ref[0].astype(F32)
    for h in range(N_HEADS):
        sl = slice((h // 2) * LANES, (h // 2 + 1) * LANES)
        keep = top_half if h % 2 == 0 else jnp.logical_not(top_half)
        qm_ref[h] = jnp.where(keep, qT[sl, :], 0.0).astype(BF16)
        qim_ref[h] = jnp.where(keep, qiT[sl, :], 0.0).astype(BF16)

    n_idx_full = q0 // CK_IDX
    key_iota = lax.broadcasted_iota(jnp.int32, (KB_IDX, TQ), 0)
    qry_iota = lax.broadcasted_iota(jnp.int32, (KB_IDX, TQ), 1)
    cpos_ref[...] = jnp.zeros(state, F32)
    cnn_ref[...] = jnp.zeros(state, F32)

    def idx_chunk(c, masked):
        off = pl.multiple_of(c * CK_IDX, CK_IDX)
        cpos = cpos_ref[...]
        cnn = cnn_ref[...]
        for r in range(CK_IDX // KB_IDX):
            koff = pl.multiple_of(off + r * KB_IDX, KB_IDX)
            kic = ki_ref[0, pl.ds(koff, KB_IDX), :]
            a = None
            for h in range(N_HEADS):
                lg = jnp.dot(kic, qim_ref[h], preferred_element_type=F32)
                t = jnp.maximum(lg, 0.0) * wT_ref[0, h:h + 1, :]
                a = t if a is None else a + t
            if masked:
                a = jnp.where(koff + key_iota <= q0 + qry_iota, a, -jnp.inf)
            isc_ref[pl.ds(koff, KB_IDX), :] = a
            isc16_ref[pl.ds(koff, KB_IDX), :] = a.astype(BF16)
            cpos = cpos + _fold(_tiles(jnp.where(a > 0.0, 1.0, 0.0)), jnp.add)
            cnn = cnn + _fold(_tiles(jnp.where(a >= 0.0, 1.0, 0.0)), jnp.add)
        cpos_ref[...] = cpos
        cnn_ref[...] = cnn

    def idx_body(c, carry):
        idx_chunk(c, False)
        return carry

    lax.fori_loop(0, n_idx_full, idx_body, 0)
    idx_chunk(n_idx_full, True)

    tail_chunks = max(CK_CNT, CK_ATT) // CK_IDX
    assert tail_chunks in (1, 2) and CK_IDX % TQ == 0

    @pl.when((n_idx_full + 1) % tail_chunks != 0)
    def _pad_tail():
        off = pl.multiple_of((n_idx_full + 1) * CK_IDX, CK_IDX)
        isc_ref[pl.ds(off, CK_IDX), :] = jnp.full((CK_IDX, TQ), -jnp.inf, F32)
        isc16_ref[pl.ds(off, CK_IDX), :] = jnp.full((CK_IDX, TQ), -jnp.inf, BF16)

    n_cnt = q0 // CK_CNT + 1
    needs_search = (q0 + lax.broadcasted_iota(jnp.int32, state, 1) + 1) > TOPK
    thr_ref[...] = jnp.full(state, LOWEST, F32)

    @pl.when(q0 + TQ > TOPK)
    def _search():
        int_min = jnp.int32(-2 ** 31)
        flip = jnp.int32(0x7FFFFFFF)
        high_bits = jnp.int32(-(1 << (32 - COARSE_BITS)))
        half_bucket = 1 << (31 - COARSE_BITS)

        def to_float(u):
            key = u ^ int_min
            return lax.bitcast_convert_type(jnp.where(key >= 0, key, key ^ flip), F32)

        def from_float(f):
            bits = lax.bitcast_convert_type(f, jnp.int32)
            return jnp.where(bits >= 0, bits, bits ^ flip) ^ int_min

        def bucket_value(u):
            bits = lax.bitcast_convert_type(to_float(u), jnp.int32) & high_bits
            return lax.bitcast_convert_type(bits, F32)

        def key_total(x):
            return replicate(jnp.sum(x, axis=0, keepdims=True))

        def count(src_ref, thr_tile, strict=False):
            rows = thr_tile.shape[0]
            one = jnp.ones((), thr_tile.dtype)
            zero = jnp.zeros((), thr_tile.dtype)

            def cnt_body(c, acc):
                off = pl.multiple_of(c * CK_CNT, CK_CNT)
                part = None
                for r in range(CK_CNT // KB_CNT):
                    koff = pl.multiple_of(off + r * KB_CNT, KB_CNT)
                    blk = _tiles(src_ref[pl.ds(koff, KB_CNT), :], rows)
                    hit = blk > thr_tile[None] if strict else blk >= thr_tile[None]
                    ones = _fold(jnp.where(hit, one, zero), jnp.add)
                    part = ones if part is None else part + ones
                return acc + part.astype(F32)

            acc = lax.fori_loop(0, n_cnt, cnt_body, jnp.zeros((rows, TQ), F32))
            return key_total(acc)

        zero_tie = jnp.logical_and(
            needs_search, jnp.logical_and(key_total(cpos_ref[...]) < float(TOPK),
                                          key_total(cnn_ref[...]) >= float(TOPK)))
        lo_ref[...] = jnp.zeros(state, jnp.int32)
        tie_ref[...] = jnp.where(zero_tie, 1.0, 0.0)
        thr_ref[...] = jnp.where(zero_tie, 0.0, LOWEST)
        done_ref[...] = jnp.where(
            jnp.logical_and(needs_search, jnp.logical_not(zero_tie)), 0.0, 1.0)

        def coarse_body(step, carry):
            cand = lo_ref[...] | lax.shift_right_logical(int_min, step)
            bucket = bucket_value(cand)
            thr_tile = jnp.concatenate([bucket, bucket], axis=0).astype(BF16)
            cnt = count(isc16_ref, thr_tile)
            lo_ref[...] = jnp.where(cnt >= float(TOPK), cand, lo_ref[...])
            return carry

        lax.fori_loop(0, COARSE_BITS, coarse_body, 0)

        lo_ref[...] = from_float(bucket_value(lo_ref[...])) - half_bucket
        width_ref[...] = jnp.full(state, 3 * half_bucket, jnp.int32)

        def cond(carry):
            step, active = carry
            return jnp.logical_and(step < COARSE_BITS + 4, active > 0.0)

        def fine_body(carry):
            step, _ = carry
            lo = lo_ref[...]
            width = width_ref[...]
            done = done_ref[...]
            half = lax.shift_right_logical(width, 1)
            mid = lo + half
            thr = to_float(mid)
            cnt = count(isc_ref, thr)
            active = done == 0.0
            ge = cnt >= float(TOPK)
            new_lo = jnp.where(ge, mid, lo)
            new_width = jnp.where(ge, width - half, half)
            hit = jnp.logical_and(active, cnt == float(TOPK))
            closed = jnp.logical_and(
                active, jnp.logical_and(jnp.logical_not(hit), new_width <= 1))
            thr_ref[...] = jnp.where(
                hit, thr, jnp.where(closed, to_float(new_lo), thr_ref[...]))
            tie_ref[...] = jnp.where(closed, 1.0, tie_ref[...])
            lo_ref[...] = jnp.where(active, new_lo, lo)
            width_ref[...] = jnp.where(active, new_width, width)
            done = jnp.where(jnp.logical_or(hit, closed), 1.0, done)
            done_ref[...] = done
            return step + 1, jnp.max(1.0 - done)

        lax.while_loop(cond, fine_body, (jnp.int32(0), jnp.max(1.0 - done_ref[...])))

        @pl.when(jnp.max(tie_ref[...]) > 0.0)
        def _ties():
            thr = thr_ref[...]
            quota = jnp.where(tie_ref[...] > 0.0,
                              float(TOPK) - count(isc_ref, thr, strict=True), 3.0e38)
            sq_r = lax.broadcasted_iota(jnp.int32, (KB_TIE, KB_TIE), 0)
            sq_c = lax.broadcasted_iota(jnp.int32, (KB_TIE, KB_TIE), 1)
            prefix = jnp.where(sq_c <= sq_r, 1.0, 0.0).astype(BF16)
            total = jnp.ones((SUBLANES, KB_TIE), BF16)

            def tie_body(c, seen):
                off = pl.multiple_of(c * KB_TIE, KB_TIE)
                blk = _tiles(isc_ref[pl.ds(off, KB_TIE), :])
                tied = blk == thr[None]
                tied_b = jnp.where(tied, 1.0, 0.0).reshape(KB_TIE, TQ).astype(BF16)
                rank = seen[None] + _tiles(
                    jnp.dot(prefix, tied_b, preferred_element_type=F32))
                drop = jnp.logical_and(tied, rank > quota[None])
                isc_ref[pl.ds(off, KB_TIE), :] = jnp.where(
                    drop, -jnp.inf, blk).reshape(KB_TIE, TQ)
                return seen + jnp.dot(total, tied_b, preferred_element_type=F32)

            lax.fori_loop(0, n_cnt * (CK_CNT // KB_TIE), tie_body, jnp.zeros(state, F32))

    m_ref[...] = jnp.full(m_ref.shape, NEG_BIG, F32)
    l_ref[...] = jnp.zeros(l_ref.shape, F32)
    acc_ref[...] = jnp.zeros(acc_ref.shape, F32)
    n_att = q0 // CK_ATT + 1

    def att_body(c, carry):
        off = pl.multiple_of(c * CK_ATT, CK_ATT)
        thr = thr_ref[...]
        blocks = [slice(r * KB_ATT, (r + 1) * KB_ATT) for r in range(CK_ATT // KB_ATT)]
        for rs in blocks:
            rows = pl.ds(pl.multiple_of(off + rs.start, KB_ATT), KB_ATT)
            bias_ref[rs, :] = jnp.where(_tiles(isc_ref[rows, :]) >= thr[None],
                                        0.0, NEG_BIG).reshape(KB_ATT, TQ)
        for h in range(N_HEADS):
            j = h // 2
            tile_max = None
            for rs in blocks:
                rows = pl.ds(pl.multiple_of(off + rs.start, KB_ATT), KB_ATT)
                kk = k_ref[0, rows, j * LANES:(j + 1) * LANES]
                s = jnp.dot(kk, qm_ref[h], preferred_element_type=F32) + bias_ref[rs, :]
                s_ref[h, rs, :] = s
                mx = _fold(_tiles(s), jnp.maximum)
                tile_max = mx if tile_max is None else jnp.maximum(tile_max, mx)
            m_old = m_ref[h]
            m_new = jnp.maximum(m_old, replicate(jnp.max(tile_max, axis=0, keepdims=True)))
            alpha_ref[h] = jnp.exp2(m_old - m_new)
            m_ref[h] = m_new
        for h in range(N_HEADS):
            m_new = m_ref[h]
            psum = None
            for rs in blocks:
                p = jnp.exp2(_tiles(s_ref[h, rs, :]) - m_new[None])
                ps = _fold(p, jnp.add)
                psum = ps if psum is None else psum + ps
                p_ref[h, rs, :] = p.reshape(KB_ATT, TQ).astype(BF16)
            l_ref[h] = alpha_ref[h] * l_ref[h] + psum
        for h in range(N_HEADS):
            j = h // 2
            vv = vT_ref[0, j * LANES:(j + 1) * LANES, pl.ds(off, CK_ATT)]
            pv = jnp.dot(vv, p_ref[h], preferred_element_type=F32)
            acc_ref[h] = alpha_ref[h][0:1, :] * acc_ref[h] + pv
        return carry

    lax.fori_loop(0, n_att, att_body, 0)

    def normalized(h):
        return acc_ref[h] / jnp.sum(l_ref[h], axis=0, keepdims=True)

    for j in range(N_PAIRS):
        out_t = jnp.where(top_half, normalized(2 * j), normalized(2 * j + 1))
        o_ref[0, :, j * LANES:(j + 1) * LANES] = out_t.T


def _attn_call(qT, qiT, wT, k, vT, ki):
    b, s, _ = k.shape
    tok = lambda bi, ti: (bi, ti, 0)
    tok_t = lambda bi, ti: (bi, 0, ti)
    per_batch = lambda bi, ti: (bi, 0, 0)
    resident = functools.partial(pl.BlockSpec, pipeline_mode=pl.Buffered(1))
    state = pltpu.VMEM((SUBLANES, TQ), F32)
    state_i = pltpu.VMEM((SUBLANES, TQ), jnp.int32)
    per_head = pltpu.VMEM((N_HEADS, SUBLANES, TQ), F32)
    return pl.pallas_call(
        _attn_kernel,
        grid=(b, s // TQ),
        in_specs=[pl.BlockSpec((1, WIDTH, TQ), tok_t),
                  pl.BlockSpec((1, WIDTH, TQ), tok_t),
                  pl.BlockSpec((1, N_HEADS, TQ), tok_t),
                  resident((1, s, WIDTH), per_batch),
                  resident((1, WIDTH, s), per_batch),
                  resident((1, s, LANES), per_batch)],
        out_specs=pl.BlockSpec((1, TQ, WIDTH), tok),
        out_shape=jax.ShapeDtypeStruct((b, s, WIDTH), F32),
        scratch_shapes=[pltpu.VMEM((s, TQ), F32),
                        pltpu.VMEM((s, TQ), BF16),
                        pltpu.VMEM((N_HEADS, LANES, TQ), BF16),
                        pltpu.VMEM((N_HEADS, LANES, TQ), BF16),
                        state,
                        state,
                        state,
                        state_i,
                        state_i,
                        state,
                        state,
                        pltpu.VMEM((CK_ATT, TQ), F32),
                        pltpu.VMEM((N_HEADS, CK_ATT, TQ), F32),
                        pltpu.VMEM((N_HEADS, CK_ATT, TQ), BF16),
                        per_head,
                        per_head,
                        per_head,
                        pltpu.VMEM((N_HEADS, LANES, TQ), F32)],
        compiler_params=pltpu.CompilerParams(
            dimension_semantics=("parallel", "arbitrary"), vmem_limit_bytes=VMEM_LIMIT),
        name="dsa_attention",
    )(qT, qiT, wT, k, vT, ki)


def _out_kernel(x_ref, a_ref, gmix_ref, g1_ref, sc2_ref, sh2_ref, g2_ref,
                ba_ref, bg_ref, n2_ref, fg_ref, wo_ref, w1_ref, w2_ref, o_ref,
                *, apply_final):
    def rms(t, g):
        return t * lax.rsqrt(jnp.mean(t * t, axis=-1, keepdims=True) + EPS) * g

    x = x_ref[0]
    a = rms(a_ref[0], ba_ref[...]).astype(BF16)
    g = rms(gmix_ref[0], bg_ref[...]).astype(BF16)
    y = (jnp.dot(a, wo_ref[0:WIDTH, :], preferred_element_type=F32)
         + jnp.dot(g, wo_ref[WIDTH:2 * WIDTH, :], preferred_element_type=F32))
    x1 = x + g1_ref[0] * y
    h2 = (rms(x1, n2_ref[...]) * (1.0 + sc2_ref[0]) + sh2_ref[0]).astype(BF16)
    ff = None
    d_ff = w1_ref.shape[1]
    for f in range(d_ff // FF_CHUNK):
        fs = slice(f * FF_CHUNK, (f + 1) * FF_CHUNK)
        u = jnp.maximum(jnp.dot(h2, w1_ref[:, fs], preferred_element_type=F32), 0.0)
        part = jnp.dot((u * u).astype(BF16), w2_ref[fs, :], preferred_element_type=F32)
        ff = part if ff is None else ff + part
    x2 = x1 + g2_ref[0] * ff
    o_ref[0] = rms(x2, fg_ref[...]) if apply_final else x2


def _out_call(x, attn, gmix, g1, sc2, sh2, g2, ba, bg, n2, fg, wo, w1, w2, apply_final):
    b, s, d = x.shape
    tm = TM_OUT
    tok = lambda bi, ti: (bi, ti, 0)
    per_b = lambda bi, ti: (bi, 0, 0)
    const2 = lambda bi, ti: (0, 0)
    resident = functools.partial(pl.BlockSpec, pipeline_mode=pl.Buffered(1))
    vec_b = pl.BlockSpec((1, 1, d), per_b)
    return pl.pallas_call(
        functools.partial(_out_kernel, apply_final=apply_final),
        grid=(b, s // tm),
        in_specs=[pl.BlockSpec((1, tm, d), tok),
                  pl.BlockSpec((1, tm, WIDTH), tok),
                  pl.BlockSpec((1, tm, WIDTH), tok),
                  vec_b, vec_b, vec_b, vec_b,
                  pl.BlockSpec((1, WIDTH), const2),
                  pl.BlockSpec((1, WIDTH), const2),
                  pl.BlockSpec((1, d), const2),
                  pl.BlockSpec((1, d), const2),
                  resident(wo.shape, const2),
                  resident(w1.shape, const2),
                  resident(w2.shape, const2)],
        out_specs=pl.BlockSpec((1, tm, d), tok),
        out_shape=jax.ShapeDtypeStruct((b, s, d), F32),
        compiler_params=pltpu.CompilerParams(
            dimension_semantics=("parallel", "parallel"), vmem_limit_bytes=VMEM_LIMIT),
        name="outproj_mlp",
    )(x, attn, gmix, g1, sc2, sh2, g2, ba, bg, n2, fg, wo, w1, w2)


def _rope_tables(s):
    half = HEAD_DIM // 2
    inv_freq = ROPE_THETA ** (-jnp.arange(half, dtype=F32) / half)
    ang = jnp.arange(s).astype(F32)[:, None] * inv_freq[None, :]
    cos = jnp.cos(ang)
    sin = jnp.sin(ang)
    cos_t = jnp.tile(jnp.concatenate([cos, cos], axis=-1), (1, LANES // HEAD_DIM))
    sin_t = jnp.tile(jnp.concatenate([-sin, sin], axis=-1), (1, LANES // HEAD_DIM))
    return cos_t, sin_t


def kernel(x, c, ada_w, ada_b, norm1_g, w_in, w_spatial, b_spatial, gm_norm_g,
           beta_attn, beta_gmlp, w_out, norm2_g, w_ff1, w_ff2, final_g):
    b, s, d = x.shape
    depth = ada_w.shape[0]
    assert d == 2 * WIDTH and w_in.shape[2] == 6 * WIDTH + HEAD_DIM + N_HEADS
    cos_t, sin_t = _rope_tables(s)
    c_pad = jnp.zeros((8, d), F32).at[:b].set(c)
    kw0 = 4 * WIDTH
    kw1 = kw0 + HEAD_DIM + N_HEADS
    for l in range(depth):
        mod = _mod_call(c_pad, ada_w[l], ada_b[l][None, :])[:b]
        sh1, sc1, g1, sh2, sc2, g2 = [m[:, None, :] for m in jnp.split(mod, 6, axis=-1)]
        w_main = jnp.concatenate([w_in[l][:, :kw0], w_in[l][:, kw1:]], axis=1).astype(BF16)
        w_kw = jnp.pad(w_in[l][:, kw0:kw1], ((0, 0), (0, LANES - (kw1 - kw0)))).astype(BF16)
        bsp = jnp.repeat(b_spatial[l].reshape(N_PAIRS, 2, CHUNK).transpose(0, 2, 1),
                         HEAD_DIM, axis=2)
        qT, k, vT, qiT, ki, wT, gmix = _inproj_call(
            x, sc1, sh1, norm1_g[l][None, :], w_main, w_kw, cos_t, sin_t,
            w_spatial[l], bsp, gm_norm_g[l][None, :])
        attn = _attn_call(qT, qiT, wT, k, vT, ki)
        x = _out_call(x, attn, gmix, g1, sc2, sh2, g2,
                      beta_attn[l][None, :], beta_gmlp[l][None, :], norm2_g[l][None, :],
                      final_g[None, :], w_out[l].astype(BF16), w_ff1[l].astype(BF16),
                      w_ff2[l].astype(BF16), apply_final=(l == depth - 1))
    return x
```

```python
import functools

import jax
import jax.numpy as jnp
from jax import lax
from jax.experimental import pallas as pl
from jax.experimental.pallas import tpu as pltpu

F32 = jnp.float32
BF16 = jnp.bfloat16

LANES = 128
HEAD_DIM = 64
N_HEADS = 8
N_PAIRS = N_HEADS // 2
WIDTH = N_HEADS * HEAD_DIM
CHUNK = 128
TOPK = 256
ROPE_THETA = 10000.0
EPS = 1e-6
LOG2E = 1.4426950408889634
NEG_BIG = -1e30
LOWEST = -3.0e38
VMEM_LIMIT = 56 * 1024 * 1024

TM_IN = 256
TQ = 256
SUBLANES = 8
KB_IDX = 128
CK_IDX = 256
KB_TIE = 256
COARSE_BITS = 16
CK_CNT = 512
KB_CNT = 128
CK_ATT = 512
KB_ATT = 128
V_ROWS = 80
TM_OUT = 512
FF_CHUNK = 1024


def _gelu_tanh(x):
    return 0.5 * x * (1.0 + jnp.tanh(0.7978845608028654 * (x + 0.044715 * x * x * x)))


def _mod_kernel(c_ref, w_ref, b_ref, o_ref):
    c = c_ref[...]
    ca = c / (1.0 + jnp.exp(-c))
    w = w_ref[...]
    ca_hi = ca.astype(BF16)
    ca_lo = (ca - ca_hi.astype(F32)).astype(BF16)
    w_hi = w.astype(BF16)
    w_lo = (w - w_hi.astype(F32)).astype(BF16)
    dot = functools.partial(jnp.dot, preferred_element_type=F32)
    o_ref[...] = dot(ca_hi, w_hi) + (dot(ca_hi, w_lo) + dot(ca_lo, w_hi)) + b_ref[...]


def _mod_call(c_pad, w, b):
    rows, d = c_pad.shape
    n = w.shape[1]
    tn = 1024
    return pl.pallas_call(
        _mod_kernel,
        grid=(n // tn,),
        in_specs=[pl.BlockSpec((rows, d), lambda j: (0, 0)),
                  pl.BlockSpec((d, tn), lambda j: (0, j)),
                  pl.BlockSpec((1, tn), lambda j: (0, j))],
        out_specs=pl.BlockSpec((rows, tn), lambda j: (0, j)),
        out_shape=jax.ShapeDtypeStruct((rows, n), F32),
        name="adaln_mod",
    )(c_pad, w, b)


def _inproj_kernel(x_ref, sc_ref, sh_ref, g_ref, wm_ref, wk_ref, cos_ref, sin_ref,
                   wsp_ref, bsp_ref, gmg_ref,
                   qT_ref, k_ref, vT_ref, qiT_ref, ki_ref, wT_ref, gm_ref):
    tm = x_ref.shape[1]
    x = x_ref[0]
    ms = jnp.mean(x * x, axis=-1, keepdims=True)
    h = x * lax.rsqrt(ms + EPS) * g_ref[...]
    h = h * (1.0 + sc_ref[0]) + sh_ref[0]
    hb = h.astype(BF16)

    cos = cos_ref[...]
    sin = sin_ref[...]
    lane = lax.broadcasted_iota(jnp.int32, (tm, LANES), 1)
    first_half = (lane & (HEAD_DIM // 2)) == 0
    low_head = lane < HEAD_DIM

    def rope(t):
        partner = jnp.where(first_half, pltpu.roll(t, LANES - HEAD_DIM // 2, 1),
                            pltpu.roll(t, HEAD_DIM // 2, 1))
        return t * cos + partner * sin

    def proj(col, width=WIDTH):
        return jnp.dot(hb, wm_ref[:, col:col + width], preferred_element_type=F32)

    pq = proj(0)
    for j in range(N_PAIRS):
        sl = slice(j * LANES, (j + 1) * LANES)
        qT_ref[0, sl, :] = (rope(pq[:, sl]) * (HEAD_DIM ** -0.5 * LOG2E)).T.astype(BF16)
    pk = proj(WIDTH)
    for j in range(N_PAIRS):
        sl = slice(j * LANES, (j + 1) * LANES)
        k_ref[0, :, sl] = rope(pk[:, sl]).astype(BF16)
    pv = proj(2 * WIDTH)
    ones_row = jnp.where(lane == HEAD_DIM, 1.0, 0.0)
    for j in range(N_PAIRS):
        pair = pv[:, j * LANES:(j + 1) * LANES]
        vT_ref[0, 2 * j] = jnp.where(low_head, pair, ones_row).T[0:V_ROWS].astype(BF16)
        vT_ref[0, 2 * j + 1] = jnp.where(
            low_head, pltpu.roll(pair, HEAD_DIM, 1), ones_row).T[0:V_ROWS].astype(BF16)
    pqi = proj(3 * WIDTH)
    for j in range(N_PAIRS):
        sl = slice(j * LANES, (j + 1) * LANES)
        qiT_ref[0, sl, :] = (rope(pqi[:, sl]) * (HEAD_DIM ** -0.5)).T.astype(BF16)

    pkw = jnp.dot(hb, wk_ref[...], preferred_element_type=F32)
    rk = rope(pkw)
    ki_ref[0] = jnp.where(low_head, rk, pltpu.roll(rk, HEAD_DIM, 1)).astype(BF16)
    wT_ref[0] = pkw.T[HEAD_DIM:HEAD_DIM + N_HEADS, :] * (N_HEADS ** -0.5)

    gu = proj(4 * WIDTH)
    gv = proj(5 * WIDTH)
    lane_c = lax.broadcasted_iota(jnp.int32, (CHUNK, LANES), 1)
    row_c = lax.broadcasted_iota(jnp.int32, (CHUNK, LANES), 0)
    low_c = lane_c < HEAD_DIM
    causal = lane_c <= row_c
    inv_n = 1.0 / HEAD_DIM
    for j in range(N_PAIRS):
        sl = slice(j * LANES, (j + 1) * LANES)
        w_lo = jnp.where(causal, wsp_ref[2 * j], 0.0).astype(BF16)
        w_hi = jnp.where(causal, wsp_ref[2 * j + 1], 0.0).astype(BF16)
        gain = gmg_ref[:, sl]
        bias = bsp_ref[j]
        for cidx in range(tm // CHUNK):
            rs = slice(cidx * CHUNK, (cidx + 1) * CHUNK)
            vv = _gelu_tanh(gv[rs, sl])
            s_all = jnp.sum(vv, axis=-1, keepdims=True)
            s_lo = jnp.sum(jnp.where(low_c, vv, 0.0), axis=-1, keepdims=True)
            mu = jnp.where(low_c, s_lo, s_all - s_lo) * inv_n
            dv = vv - mu
            d2 = dv * dv
            q_all = jnp.sum(d2, axis=-1, keepdims=True)
            q_lo = jnp.sum(jnp.where(low_c, d2, 0.0), axis=-1, keepdims=True)
            var = jnp.where(low_c, q_lo, q_all - q_lo) * inv_n
            vn = (dv * lax.rsqrt(var + EPS) * gain).astype(BF16)
            m_lo = jnp.dot(w_lo, vn, preferred_element_type=F32)
            m_hi = jnp.dot(w_hi, vn, preferred_element_type=F32)
            mixed = jnp.where(low_c, m_lo, m_hi) + bias
            gm_ref[0, rs, sl] = _gelu_tanh(gu[rs, sl]) * mixed


def _inproj_call(x, sc1, sh1, g1n, w_main, w_kw, cos_t, sin_t, wsp, bsp, gmg):
    b, s, d = x.shape
    tm = TM_IN
    const2 = lambda bi, ti: (0, 0)
    const3 = lambda bi, ti: (0, 0, 0)
    tok = lambda bi, ti: (bi, ti, 0)
    tok_t = lambda bi, ti: (bi, 0, ti)
    return pl.pallas_call(
        _inproj_kernel,
        grid=(b, s // tm),
        in_specs=[pl.BlockSpec((1, tm, d), tok),
                  pl.BlockSpec((1, 1, d), lambda bi, ti: (bi, 0, 0)),
                  pl.BlockSpec((1, 1, d), lambda bi, ti: (bi, 0, 0)),
                  pl.BlockSpec((1, d), const2),
                  pl.BlockSpec(w_main.shape, const2),
                  pl.BlockSpec(w_kw.shape, const2),
                  pl.BlockSpec((tm, LANES), lambda bi, ti: (ti, 0)),
                  pl.BlockSpec((tm, LANES), lambda bi, ti: (ti, 0)),
                  pl.BlockSpec(wsp.shape, const3),
                  pl.BlockSpec(bsp.shape, const3),
                  pl.BlockSpec(gmg.shape, const2)],
        out_specs=[pl.BlockSpec((1, WIDTH, tm), tok_t),
                   pl.BlockSpec((1, tm, WIDTH), tok),
                   pl.BlockSpec((1, N_HEADS, V_ROWS, tm), lambda bi, ti: (bi, 0, 0, ti)),
                   pl.BlockSpec((1, WIDTH, tm), tok_t),
                   pl.BlockSpec((1, tm, LANES), tok),
                   pl.BlockSpec((1, N_HEADS, tm), tok_t),
                   pl.BlockSpec((1, tm, WIDTH), tok)],
        out_shape=[jax.ShapeDtypeStruct((b, WIDTH, s), BF16),
                   jax.ShapeDtypeStruct((b, s, WIDTH), BF16),
                   jax.ShapeDtypeStruct((b, N_HEADS, V_ROWS, s), BF16),
                   jax.ShapeDtypeStruct((b, WIDTH, s), BF16),
                   jax.ShapeDtypeStruct((b, s, LANES), BF16),
                   jax.ShapeDtypeStruct((b, N_HEADS, s), F32),
                   jax.ShapeDtypeStruct((b, s, WIDTH), F32)],
        compiler_params=pltpu.CompilerParams(
            dimension_semantics=("parallel", "parallel"), vmem_limit_bytes=VMEM_LIMIT),
        name="inproj_gmlp",
    )(x, sc1, sh1, g1n, w_main, w_kw, cos_t, sin_t, wsp, bsp, gmg)


def _tiles(x, rows=SUBLANES):
    return x.reshape(x.shape[0] // rows, rows, x.shape[1])


def _fold(x3, op, chains=4):
    accs = [x3[i] for i in range(min(chains, x3.shape[0]))]
    for i in range(len(accs), x3.shape[0]):
        accs[i % chains] = op(accs[i % chains], x3[i])
    while len(accs) > 1:
        accs = [op(accs[i], accs[i + 1]) if i + 1 < len(accs) else accs[i]
                for i in range(0, len(accs), 2)]
    return accs[0]


def _attn_kernel(qT_ref, qiT_ref, wT_ref, k_ref, vT_ref, ki_ref, o_ref,
                 isc_ref, isc16_ref, qm_ref, qim_ref,
                 thr_ref, lo_ref, width_ref, done_ref, tie_ref,
                 bias_ref, s_ref, p_ref, alpha_ref, m_ref, acc_ref):
    i = pl.program_id(1)
    q0 = i * TQ
    state = (SUBLANES, TQ)
    top_half = lax.broadcasted_iota(jnp.int32, (LANES, TQ), 0) < HEAD_DIM

    def replicate(row):
        return jnp.broadcast_to(row, state)

    qT = qT_ref[0].astype(F32)
    qiT = qiT_ref[0].astype(F32)
    for h in range(N_HEADS):
        sl = slice((h // 2) * LANES, (h // 2 + 1) * LANES)
        keep = top_half if h % 2 == 0 else jnp.logical_not(top_half)
        qm_ref[h] = jnp.where(keep, qT[sl, :], 0.0).astype(BF16)
        qim_ref[h] = jnp.where(keep, qiT[sl, :], 0.0).astype(BF16)

    n_idx_full = q0 // CK_IDX
    key_iota = lax.broadcasted_iota(jnp.int32, (KB_IDX, TQ), 0)
    qry_iota = lax.broadcasted_iota(jnp.int32, (KB_IDX, TQ), 1)

    def idx_chunk(c, masked):
        off = pl.multiple_of(c * CK_IDX, CK_IDX)
        for r in range(CK_IDX // KB_IDX):
            koff = pl.multiple_of(off + r * KB_IDX, KB_IDX)
            kic = ki_ref[0, pl.ds(koff, KB_IDX), :]
            a = None
            for h in range(N_HEADS):
                lg = jnp.dot(kic, qim_ref[h], preferred_element_type=F32)
                t = jnp.maximum(lg, 0.0) * wT_ref[0, h:h + 1, :]
                a = t if a is None else a + t
            if masked:
                a = jnp.where(koff + key_iota <= q0 + qry_iota, a, -jnp.inf)
            isc_ref[pl.ds(koff, KB_IDX), :] = a
            isc16_ref[pl.ds(koff, KB_IDX), :] = a.astype(BF16)

    def idx_body(c, carry):
        idx_chunk(c, False)
        return carry

    lax.fori_loop(0, n_idx_full, idx_body, 0)
    idx_chunk(n_idx_full, True)

    tail_chunks = max(CK_CNT, CK_ATT) // CK_IDX
    assert CK_IDX % TQ == 0 and max(CK_CNT, CK_ATT) % min(CK_CNT, CK_ATT) == 0

    for t in range(1, tail_chunks):
        @pl.when((n_idx_full + t) // tail_chunks == n_idx_full // tail_chunks)
        def _pad_tail(t=t):
            off = pl.multiple_of((n_idx_full + t) * CK_IDX, CK_IDX)
            isc_ref[pl.ds(off, CK_IDX), :] = jnp.full((CK_IDX, TQ), -jnp.inf, F32)
            isc16_ref[pl.ds(off, CK_IDX), :] = jnp.full((CK_IDX, TQ), -jnp.inf, BF16)

    n_cnt = q0 // CK_CNT + 1
    needs_search = (q0 + lax.broadcasted_iota(jnp.int32, state, 1) + 1) > TOPK
    thr_ref[...] = jnp.full(state, LOWEST, F32)

    @pl.when(q0 + TQ > TOPK)
    def _search():
        int_min = jnp.int32(-2 ** 31)
        flip = jnp.int32(0x7FFFFFFF)
        high_bits = jnp.int32(-(1 << (32 - COARSE_BITS)))
        half_bucket = 1 << (31 - COARSE_BITS)

        def to_float(u):
            key = u ^ int_min
            return lax.bitcast_convert_type(jnp.where(key >= 0, key, key ^ flip), F32)

        def from_float(f):
            bits = lax.bitcast_convert_type(f, jnp.int32)
            return jnp.where(bits >= 0, bits, bits ^ flip) ^ int_min

        def bucket_value(u):
            bits = lax.bitcast_convert_type(to_float(u), jnp.int32) & high_bits
            return lax.bitcast_convert_type(bits, F32)

        def key_total(x):
            return replicate(jnp.sum(x, axis=0, keepdims=True))

        def count(src_ref, thr_tile, strict=False):
            rows = thr_tile.shape[0]
            one = jnp.ones((), thr_tile.dtype)
            zero = jnp.zeros((), thr_tile.dtype)

            def cnt_body(c, acc):
                off = pl.multiple_of(c * CK_CNT, CK_CNT)
                part = None
                for r in range(CK_CNT // KB_CNT):
                    koff = pl.multiple_of(off + r * KB_CNT, KB_CNT)
                    blk = _tiles(src_ref[pl.ds(koff, KB_CNT), :], rows)
                    hit = blk > thr_tile[None] if strict else blk >= thr_tile[None]
                    ones = _fold(jnp.where(hit, one, zero), jnp.add)
                    part = ones if part is None else part + ones
                return acc + part.astype(F32)

            acc = lax.fori_loop(0, n_cnt, cnt_body, jnp.zeros((rows, TQ), F32))
            return key_total(acc)

        lo_ref[...] = jnp.zeros(state, jnp.int32)

        def coarse_body(step, carry):
            cand = lo_ref[...] | lax.shift_right_logical(int_min, step)
            bucket = bucket_value(cand)
            thr_tile = jnp.concatenate([bucket, bucket], axis=0).astype(BF16)
            cnt = count(isc16_ref, thr_tile)
            lo_ref[...] = jnp.where(cnt >= float(TOPK), cand, lo_ref[...])
            return carry

        lax.fori_loop(0, COARSE_BITS, coarse_body, 0)

        bucket = bucket_value(lo_ref[...])
        exponent = lax.bitcast_convert_type(bucket, jnp.int32) & jnp.int32(0x7F800000)
        zero_tie = jnp.logical_and(needs_search, exponent == 0)
        tie_ref[...] = jnp.where(zero_tie, 1.0, 0.0)
        thr_ref[...] = jnp.where(zero_tie, 0.0, LOWEST)
        done_ref[...] = jnp.where(
            jnp.logical_and(needs_search, jnp.logical_not(zero_tie)), 0.0, 1.0)

        lo_ref[...] = from_float(bucket) - half_bucket
        width_ref[...] = jnp.full(state, 3 * half_bucket, jnp.int32)

        def cond(carry):
            step, active = carry
            return jnp.logical_and(step < COARSE_BITS + 4, active > 0.0)

        def fine_body(carry):
            step, _ = carry
            lo = lo_ref[...]
            width = width_ref[...]
            done = done_ref[...]
            half = lax.shift_right_logical(width, 1)
            mid = lo + half
            thr = to_float(mid)
            cnt = count(isc_ref, thr)
            active = done == 0.0
            ge = cnt >= float(TOPK)
            new_lo = jnp.where(ge, mid, lo)
            new_width = jnp.where(ge, width - half, half)
            hit = jnp.logical_and(active, cnt == float(TOPK))
            closed = jnp.logical_and(
                active, jnp.logical_and(jnp.logical_not(hit), new_width <= 1))
            thr_ref[...] = jnp.where(
                hit, thr, jnp.where(closed, to_float(new_lo), thr_ref[...]))
            tie_ref[...] = jnp.where(closed, 1.0, tie_ref[...])
            lo_ref[...] = jnp.where(active, new_lo, lo)
            width_ref[...] = jnp.where(active, new_width, width)
            done = jnp.where(jnp.logical_or(hit, closed), 1.0, done)
            done_ref[...] = done
            return step + 1, jnp.max(1.0 - done)

        lax.while_loop(cond, fine_body, (jnp.int32(0), jnp.max(1.0 - done_ref[...])))

        @pl.when(jnp.max(tie_ref[...]) > 0.0)
        def _ties():
            thr = thr_ref[...]
            quota = jnp.where(tie_ref[...] > 0.0,
                              float(TOPK) - count(isc_ref, thr, strict=True), 3.0e38)
            sq_r = lax.broadcasted_iota(jnp.int32, (KB_TIE, KB_TIE), 0)
            sq_c = lax.broadcasted_iota(jnp.int32, (KB_TIE, KB_TIE), 1)
            prefix = jnp.where(sq_c <= sq_r, 1.0, 0.0).astype(BF16)

            def tie_body(c, seen):
                off = pl.multiple_of(c * KB_TIE, KB_TIE)
                blk = _tiles(isc_ref[pl.ds(off, KB_TIE), :])
                tied = blk == thr[None]
                tied_f = jnp.where(tied, 1.0, 0.0)
                rank = seen[None] + _tiles(jnp.dot(
                    prefix, tied_f.reshape(KB_TIE, TQ).astype(BF16),
                    preferred_element_type=F32))
                drop = jnp.logical_and(tied, rank > quota[None])
                isc_ref[pl.ds(off, KB_TIE), :] = jnp.where(
                    drop, -jnp.inf, blk).reshape(KB_TIE, TQ)
                return seen + key_total(_fold(tied_f, jnp.add))

            lax.fori_loop(0, n_cnt * (CK_CNT // KB_TIE), tie_body, jnp.zeros(state, F32))

    m_ref[...] = jnp.full(m_ref.shape, NEG_BIG, F32)
    acc_ref[...] = jnp.zeros(acc_ref.shape, F32)
    n_att = q0 // CK_ATT + 1

    def att_body(c, carry):
        off = pl.multiple_of(c * CK_ATT, CK_ATT)
        thr = thr_ref[...]
        blocks = [slice(r * KB_ATT, (r + 1) * KB_ATT) for r in range(CK_ATT // KB_ATT)]
        for rs in blocks:
            rows = pl.ds(pl.multiple_of(off + rs.start, KB_ATT), KB_ATT)
            bias_ref[rs, :] = jnp.where(_tiles(isc_ref[rows, :]) >= thr[None],
                                        0.0, NEG_BIG).reshape(KB_ATT, TQ)
        for h in range(N_HEADS):
            j = h // 2
            tile_max = None
            for rs in blocks:
                rows = pl.ds(pl.multiple_of(off + rs.start, KB_ATT), KB_ATT)
                kk = k_ref[0, rows, j * LANES:(j + 1) * LANES]
                s = jnp.dot(kk, qm_ref[h], preferred_element_type=F32) + bias_ref[rs, :]
                s_ref[h, rs, :] = s
                mx = _fold(_tiles(s), jnp.maximum)
                tile_max = mx if tile_max is None else jnp.maximum(tile_max, mx)
            m_old = m_ref[h]
            m_new = jnp.maximum(m_old, replicate(jnp.max(tile_max, axis=0, keepdims=True)))
            alpha_ref[h] = jnp.exp2(m_old - m_new)
            m_ref[h] = m_new
        for h in range(N_HEADS):
            m_new = m_ref[h]
            for rs in blocks:
                d = (_tiles(s_ref[h, rs, :]) - m_new[None]).reshape(KB_ATT, TQ)
                p_ref[h, rs, :] = jnp.exp2(d.astype(BF16))
        for h in range(N_HEADS):
            vv = vT_ref[0, h, :, pl.ds(off, CK_ATT)]
            pv = jnp.dot(vv, p_ref[h], preferred_element_type=F32)
            acc_ref[h] = alpha_ref[h][0:1, :] * acc_ref[h] + pv
        return carry

    lax.fori_loop(0, n_att, att_body, 0)

    def normalized(h):
        acc = acc_ref[h]
        return acc[0:HEAD_DIM, :] / acc[HEAD_DIM:HEAD_DIM + 1, :]

    for j in range(N_PAIRS):
        out_t = jnp.concatenate([normalized(2 * j), normalized(2 * j + 1)], axis=0)
        o_ref[0, :, j * LANES:(j + 1) * LANES] = out_t.T


def _attn_call(qT, qiT, wT, k, vT, ki):
    b, s, _ = k.shape
    tok = lambda bi, ti: (bi, ti, 0)
    tok_t = lambda bi, ti: (bi, 0, ti)
    per_batch = lambda bi, ti: (bi, 0, 0)
    resident = functools.partial(pl.BlockSpec, pipeline_mode=pl.Buffered(1))
    state = pltpu.VMEM((SUBLANES, TQ), F32)
    state_i = pltpu.VMEM((SUBLANES, TQ), jnp.int32)
    per_head = pltpu.VMEM((N_HEADS, SUBLANES, TQ), F32)
    return pl.pallas_call(
        _attn_kernel,
        grid=(b, s // TQ),
        in_specs=[pl.BlockSpec((1, WIDTH, TQ), tok_t),
                  pl.BlockSpec((1, WIDTH, TQ), tok_t),
                  pl.BlockSpec((1, N_HEADS, TQ), tok_t),
                  resident((1, s, WIDTH), per_batch),
                  resident((1, N_HEADS, V_ROWS, s), lambda bi, ti: (bi, 0, 0, 0)),
                  resident((1, s, LANES), per_batch)],
        out_specs=pl.BlockSpec((1, TQ, WIDTH), tok),
        out_shape=jax.ShapeDtypeStruct((b, s, WIDTH), F32),
        scratch_shapes=[pltpu.VMEM((s, TQ), F32),
                        pltpu.VMEM((s, TQ), BF16),
                        pltpu.VMEM((N_HEADS, LANES, TQ), BF16),
                        pltpu.VMEM((N_HEADS, LANES, TQ), BF16),
                        state,
                        state_i,
                        state_i,
                        state,
                        state,
                        pltpu.VMEM((CK_ATT, TQ), F32),
                        pltpu.VMEM((N_HEADS, CK_ATT, TQ), F32),
                        pltpu.VMEM((N_HEADS, CK_ATT, TQ), BF16),
                        per_head,
                        per_head,
                        pltpu.VMEM((N_HEADS, V_ROWS, TQ), F32)],
        compiler_params=pltpu.CompilerParams(
            dimension_semantics=("parallel", "arbitrary"), vmem_limit_bytes=VMEM_LIMIT),
        name="dsa_attention",
    )(qT, qiT, wT, k, vT, ki)


def _out_kernel(x_ref, a_ref, gmix_ref, g1_ref, sc2_ref, sh2_ref, g2_ref,
                ba_ref, bg_ref, n2_ref, fg_ref, wo_ref, w1_ref, w2_ref, o_ref,
                *, apply_final):
    def rms(t, g):
        return t * lax.rsqrt(jnp.mean(t * t, axis=-1, keepdims=True) + EPS) * g

    x = x_ref[0]
    a = rms(a_ref[0], ba_ref[...]).astype(BF16)
    g = rms(gmix_ref[0], bg_ref[...]).astype(BF16)
    y = (jnp.dot(a, wo_ref[0:WIDTH, :], preferred_element_type=F32)
         + jnp.dot(g, wo_ref[WIDTH:2 * WIDTH, :], preferred_element_type=F32))
    x1 = x + g1_ref[0] * y
    h2 = (rms(x1, n2_ref[...]) * (1.0 + sc2_ref[0]) + sh2_ref[0]).astype(BF16)
    ff = None
    d_ff = w1_ref.shape[1]
    for f in range(d_ff // FF_CHUNK):
        fs = slice(f * FF_CHUNK, (f + 1) * FF_CHUNK)
        u = jnp.maximum(jnp.dot(h2, w1_ref[:, fs], preferred_element_type=F32), 0.0)
        part = jnp.dot((u * u).astype(BF16), w2_ref[fs, :], preferred_element_type=F32)
        ff = part if ff is None else ff + part
    x2 = x1 + g2_ref[0] * ff
    o_ref[0] = rms(x2, fg_ref[...]) if apply_final else x2


def _out_call(x, attn, gmix, g1, sc2, sh2, g2, ba, bg, n2, fg, wo, w1, w2, apply_final):
    b, s, d = x.shape
    tm = TM_OUT
    tok = lambda bi, ti: (bi, ti, 0)
    per_b = lambda bi, ti: (bi, 0, 0)
    const2 = lambda bi, ti: (0, 0)
    resident = functools.partial(pl.BlockSpec, pipeline_mode=pl.Buffered(1))
    vec_b = pl.BlockSpec((1, 1, d), per_b)
    return pl.pallas_call(
        functools.partial(_out_kernel, apply_final=apply_final),
        grid=(b, s // tm),
        in_specs=[pl.BlockSpec((1, tm, d), tok),
                  pl.BlockSpec((1, tm, WIDTH), tok),
                  pl.BlockSpec((1, tm, WIDTH), tok),
                  vec_b, vec_b, vec_b, vec_b,
                  pl.BlockSpec((1, WIDTH), const2),
                  pl.BlockSpec((1, WIDTH), const2),
                  pl.BlockSpec((1, d), const2),
                  pl.BlockSpec((1, d), const2),
                  resident(wo.shape, const2),
                  resident(w1.shape, const2),
                  resident(w2.shape, const2)],
        out_specs=pl.BlockSpec((1, tm, d), tok),
        out_shape=jax.ShapeDtypeStruct((b, s, d), F32),
        compiler_params=pltpu.CompilerParams(
            dimension_semantics=("parallel", "parallel"), vmem_limit_bytes=VMEM_LIMIT),
        name="outproj_mlp",
    )(x, attn, gmix, g1, sc2, sh2, g2, ba, bg, n2, fg, wo, w1, w2)


def _rope_tables(s):
    half = HEAD_DIM // 2
    inv_freq = ROPE_THETA ** (-jnp.arange(half, dtype=F32) / half)
    ang = jnp.arange(s).astype(F32)[:, None] * inv_freq[None, :]
    cos = jnp.cos(ang)
    sin = jnp.sin(ang)
    cos_t = jnp.tile(jnp.concatenate([cos, cos], axis=-1), (1, LANES // HEAD_DIM))
    sin_t = jnp.tile(jnp.concatenate([-sin, sin], axis=-1), (1, LANES // HEAD_DIM))
    return cos_t, sin_t


def kernel(x, c, ada_w, ada_b, norm1_g, w_in, w_spatial, b_spatial, gm_norm_g,
           beta_attn, beta_gmlp, w_out, norm2_g, w_ff1, w_ff2, final_g):
    b, s, d = x.shape
    depth = ada_w.shape[0]
    assert d == 2 * WIDTH and w_in.shape[2] == 6 * WIDTH + HEAD_DIM + N_HEADS
    cos_t, sin_t = _rope_tables(s)
    c_pad = jnp.zeros((8, d), F32).at[:b].set(c)
    kw0 = 4 * WIDTH
    kw1 = kw0 + HEAD_DIM + N_HEADS
    for l in range(depth):
        mod = _mod_call(c_pad, ada_w[l], ada_b[l][None, :])[:b]
        sh1, sc1, g1, sh2, sc2, g2 = [m[:, None, :] for m in jnp.split(mod, 6, axis=-1)]
        w_main = jnp.concatenate([w_in[l][:, :kw0], w_in[l][:, kw1:]], axis=1).astype(BF16)
        w_kw = jnp.pad(w_in[l][:, kw0:kw1], ((0, 0), (0, LANES - (kw1 - kw0)))).astype(BF16)
        bsp = jnp.repeat(b_spatial[l].reshape(N_PAIRS, 2, CHUNK).transpose(0, 2, 1),
                         HEAD_DIM, axis=2)
        qT, k, vT, qiT, ki, wT, gmix = _inproj_call(
            x, sc1, sh1, norm1_g[l][None, :], w_main, w_kw, cos_t, sin_t,
            w_spatial[l], bsp, gm_norm_g[l][None, :])
        attn = _attn_call(qT, qiT, wT, k, vT, ki)
        x = _out_call(x, attn, gmix, g1, sc2, sh2, g2,
                      beta_attn[l][None, :], beta_gmlp[l][None, :], norm2_g[l][None, :],
                      final_g[None, :], w_out[l].astype(BF16), w_ff1[l].astype(BF16),
                      w_ff2[l].astype(BF16), apply_final=(l == depth - 1))
    return x
```

```python
import functools

import jax
import jax.numpy as jnp
from jax import lax
from jax.experimental import pallas as pl
from jax.experimental.pallas import tpu as pltpu

F32 = jnp.float32
BF16 = jnp.bfloat16

LANES = 128
HEAD_DIM = 64
N_HEADS = 8
N_PAIRS = N_HEADS // 2
WIDTH = N_HEADS * HEAD_DIM
CHUNK = 128
TOPK = 256
ROPE_THETA = 10000.0
EPS = 1e-6
LOG2E = 1.4426950408889634
NEG_BIG = -1e30
LOWEST = -3.0e38
VMEM_LIMIT = 56 * 1024 * 1024

TM_IN = 512
TQ = 256
SUBLANES = 8
KB_IDX = 128
CK_IDX = 512
KB_TIE = 256
COARSE_BITS = 16
CK_CNT = 512
KB_CNT = 128
CK_ATT = 512
KB_ATT = 128
V_ROWS = 80
TM_OUT = 512
FF_CHUNK = 1024


def _gelu_tanh(x):
    return 0.5 * x * (1.0 + jnp.tanh(0.7978845608028654 * (x + 0.044715 * x * x * x)))


def _mod_kernel(c_ref, w_ref, b_ref, o_ref):
    c = c_ref[...]
    ca = c / (1.0 + jnp.exp(-c))
    w = w_ref[0]
    ca_hi = ca.astype(BF16)
    ca_lo = (ca - ca_hi.astype(F32)).astype(BF16)
    w_hi = w.astype(BF16)
    w_lo = (w - w_hi.astype(F32)).astype(BF16)
    dot = functools.partial(jnp.dot, preferred_element_type=F32)
    o_ref[...] = dot(ca_hi, w_hi) + (dot(ca_hi, w_lo) + dot(ca_lo, w_hi)) + b_ref[0]


def _mod_call(c_pad, w, b, layer):
    rows, d = c_pad.shape
    n = w.shape[2]
    tn = 1024
    return pl.pallas_call(
        _mod_kernel,
        grid=(n // tn,),
        in_specs=[pl.BlockSpec((rows, d), lambda j: (0, 0)),
                  pl.BlockSpec((1, d, tn), lambda j: (layer, 0, j)),
                  pl.BlockSpec((1, 1, tn), lambda j: (layer, 0, j))],
        out_specs=pl.BlockSpec((rows, tn), lambda j: (0, j)),
        out_shape=jax.ShapeDtypeStruct((rows, n), F32),
        name="adaln_mod",
    )(c_pad, w, b)


def _inproj_kernel(x_ref, sc_ref, sh_ref, g_ref, wm_ref, wk_ref, cos_ref, sin_ref,
                   wsp_ref, bsp_ref, gmg_ref,
                   qT_ref, k_ref, vT_ref, qiT_ref, ki_ref, wT_ref, gm_ref):
    tm = x_ref.shape[1]
    x = x_ref[0]
    ms = jnp.mean(x * x, axis=-1, keepdims=True)
    h = x * lax.rsqrt(ms + EPS) * g_ref[...]
    h = h * (1.0 + sc_ref[0]) + sh_ref[0]
    hb = h.astype(BF16)

    cos = cos_ref[...]
    sin = sin_ref[...]
    lane = lax.broadcasted_iota(jnp.int32, (tm, LANES), 1)
    first_half = (lane & (HEAD_DIM // 2)) == 0
    low_head = lane < HEAD_DIM

    def rope(t):
        partner = jnp.where(first_half, pltpu.roll(t, LANES - HEAD_DIM // 2, 1),
                            pltpu.roll(t, HEAD_DIM // 2, 1))
        return t * cos + partner * sin

    def proj(col, width=WIDTH):
        return jnp.dot(hb, wm_ref[:, col:col + width], preferred_element_type=F32)

    pq = proj(0)
    for j in range(N_PAIRS):
        sl = slice(j * LANES, (j + 1) * LANES)
        qT_ref[0, sl, :] = (rope(pq[:, sl]) * (HEAD_DIM ** -0.5 * LOG2E)).T.astype(BF16)
    pk = proj(WIDTH)
    for j in range(N_PAIRS):
        sl = slice(j * LANES, (j + 1) * LANES)
        k_ref[0, :, sl] = rope(pk[:, sl]).astype(BF16)
    pv = proj(2 * WIDTH)
    ones_row = jnp.where(lane == HEAD_DIM, 1.0, 0.0)
    for j in range(N_PAIRS):
        pair = pv[:, j * LANES:(j + 1) * LANES]
        vT_ref[0, 2 * j] = jnp.where(low_head, pair, ones_row).T[0:V_ROWS].astype(BF16)
        vT_ref[0, 2 * j + 1] = jnp.where(
            low_head, pltpu.roll(pair, HEAD_DIM, 1), ones_row).T[0:V_ROWS].astype(BF16)
    pqi = proj(3 * WIDTH)
    for j in range(N_PAIRS):
        sl = slice(j * LANES, (j + 1) * LANES)
        qiT_ref[0, sl, :] = (rope(pqi[:, sl]) * (HEAD_DIM ** -0.5)).T.astype(BF16)

    pkw = jnp.dot(hb, wk_ref[...], preferred_element_type=F32)
    rk = rope(pkw)
    ki_ref[0] = jnp.where(low_head, rk, pltpu.roll(rk, HEAD_DIM, 1)).astype(BF16)
    wT_ref[0] = pkw.T[HEAD_DIM:HEAD_DIM + N_HEADS, :] * (N_HEADS ** -0.5)

    gu = proj(4 * WIDTH)
    gv = proj(5 * WIDTH)
    lane_c = lax.broadcasted_iota(jnp.int32, (CHUNK, LANES), 1)
    row_c = lax.broadcasted_iota(jnp.int32, (CHUNK, LANES), 0)
    low_c = lane_c < HEAD_DIM
    causal = lane_c <= row_c
    inv_n = 1.0 / HEAD_DIM
    for j in range(N_PAIRS):
        sl = slice(j * LANES, (j + 1) * LANES)
        w_lo = jnp.where(causal, wsp_ref[2 * j], 0.0).astype(BF16)
        w_hi = jnp.where(causal, wsp_ref[2 * j + 1], 0.0).astype(BF16)
        gain = gmg_ref[:, sl]
        bias = bsp_ref[j]
        for cidx in range(tm // CHUNK):
            rs = slice(cidx * CHUNK, (cidx + 1) * CHUNK)
            vv = _gelu_tanh(gv[rs, sl])
            s_all = jnp.sum(vv, axis=-1, keepdims=True)
            s_lo = jnp.sum(jnp.where(low_c, vv, 0.0), axis=-1, keepdims=True)
            mu = jnp.where(low_c, s_lo, s_all - s_lo) * inv_n
            dv = vv - mu
            d2 = dv * dv
            q_all = jnp.sum(d2, axis=-1, keepdims=True)
            q_lo = jnp.sum(jnp.where(low_c, d2, 0.0), axis=-1, keepdims=True)
            var = jnp.where(low_c, q_lo, q_all - q_lo) * inv_n
            vn = (dv * lax.rsqrt(var + EPS) * gain).astype(BF16)
            m_lo = jnp.dot(w_lo, vn, preferred_element_type=F32)
            m_hi = jnp.dot(w_hi, vn, preferred_element_type=F32)
            mixed = jnp.where(low_c, m_lo, m_hi) + bias
            gm_ref[0, rs, sl] = _gelu_tanh(gu[rs, sl]) * mixed


def _inproj_call(x, sc1, sh1, g1n, w_main, w_kw, cos_t, sin_t, wsp, bsp, gmg):
    b, s, d = x.shape
    tm = TM_IN
    const2 = lambda bi, ti: (0, 0)
    const3 = lambda bi, ti: (0, 0, 0)
    tok = lambda bi, ti: (bi, ti, 0)
    tok_t = lambda bi, ti: (bi, 0, ti)
    return pl.pallas_call(
        _inproj_kernel,
        grid=(b, s // tm),
        in_specs=[pl.BlockSpec((1, tm, d), tok),
                  pl.BlockSpec((1, 1, d), lambda bi, ti: (bi, 0, 0)),
                  pl.BlockSpec((1, 1, d), lambda bi, ti: (bi, 0, 0)),
                  pl.BlockSpec((1, d), const2),
                  pl.BlockSpec(w_main.shape, const2),
                  pl.BlockSpec(w_kw.shape, const2),
                  pl.BlockSpec((tm, LANES), lambda bi, ti: (ti, 0)),
                  pl.BlockSpec((tm, LANES), lambda bi, ti: (ti, 0)),
                  pl.BlockSpec(wsp.shape, const3),
                  pl.BlockSpec(bsp.shape, const3),
                  pl.BlockSpec(gmg.shape, const2)],
        out_specs=[pl.BlockSpec((1, WIDTH, tm), tok_t),
                   pl.BlockSpec((1, tm, WIDTH), tok),
                   pl.BlockSpec((1, N_HEADS, V_ROWS, tm), lambda bi, ti: (bi, 0, 0, ti)),
                   pl.BlockSpec((1, WIDTH, tm), tok_t),
                   pl.BlockSpec((1, tm, LANES), tok),
                   pl.BlockSpec((1, N_HEADS, tm), tok_t),
                   pl.BlockSpec((1, tm, WIDTH), tok)],
        out_shape=[jax.ShapeDtypeStruct((b, WIDTH, s), BF16),
                   jax.ShapeDtypeStruct((b, s, WIDTH), BF16),
                   jax.ShapeDtypeStruct((b, N_HEADS, V_ROWS, s), BF16),
                   jax.ShapeDtypeStruct((b, WIDTH, s), BF16),
                   jax.ShapeDtypeStruct((b, s, LANES), BF16),
                   jax.ShapeDtypeStruct((b, N_HEADS, s), F32),
                   jax.ShapeDtypeStruct((b, s, WIDTH), F32)],
        compiler_params=pltpu.CompilerParams(
            dimension_semantics=("parallel", "parallel"), vmem_limit_bytes=VMEM_LIMIT),
        name="inproj_gmlp",
    )(x, sc1, sh1, g1n, w_main, w_kw, cos_t, sin_t, wsp, bsp, gmg)


def _tiles(x, rows=SUBLANES):
    return x.reshape(x.shape[0] // rows, rows, x.shape[1])


def _fold(x3, op, chains=4):
    accs = [x3[i] for i in range(min(chains, x3.shape[0]))]
    for i in range(len(accs), x3.shape[0]):
        accs[i % chains] = op(accs[i % chains], x3[i])
    while len(accs) > 1:
        accs = [op(accs[i], accs[i + 1]) if i + 1 < len(accs) else accs[i]
                for i in range(0, len(accs), 2)]
    return accs[0]


def _attn_kernel(qT_ref, qiT_ref, wT_ref, k_ref, vT_ref, ki_ref, o_ref,
                 isc_ref, isc16_ref, qm_ref, qim_ref,
                 thr_ref, lo_ref, width_ref, done_ref, tie_ref,
                 bias_ref, s_ref, p_ref, alpha_ref, m_ref, acc_ref):
    i = pl.program_id(1)
    q0 = i * TQ
    state = (SUBLANES, TQ)
    top_half = lax.broadcasted_iota(jnp.int32, (LANES, TQ), 0) < HEAD_DIM

    def replicate(row):
        return jnp.broadcast_to(row, state)

    qT = qT_ref[0].astype(F32)
    qiT = qiT_ref[0].astype(F32)
    for h in range(N_HEADS):
        sl = slice((h // 2) * LANES, (h // 2 + 1) * LANES)
        keep = top_half if h % 2 == 0 else jnp.logical_not(top_half)
        qm_ref[h] = jnp.where(keep, qT[sl, :], 0.0).astype(BF16)
        qim_ref[h] = jnp.where(keep, qiT[sl, :], 0.0).astype(BF16)

    n_idx_full = q0 // CK_IDX
    key_iota = lax.broadcasted_iota(jnp.int32, (KB_IDX, TQ), 0)
    qry_iota = lax.broadcasted_iota(jnp.int32, (KB_IDX, TQ), 1)

    def idx_keys(off, nkeys, masked):
        for r in range(nkeys // KB_IDX):
            koff = pl.multiple_of(off + r * KB_IDX, KB_IDX)
            kic = ki_ref[0, pl.ds(koff, KB_IDX), :]
            a = None
            for h in range(N_HEADS):
                lg = jnp.dot(kic, qim_ref[h], preferred_element_type=F32)
                t = jnp.maximum(lg, 0.0) * wT_ref[0, h:h + 1, :]
                a = t if a is None else a + t
            if masked:
                a = jnp.where(koff + key_iota <= q0 + qry_iota, a, -jnp.inf)
            isc_ref[pl.ds(koff, KB_IDX), :] = a
            isc16_ref[pl.ds(koff, KB_IDX), :] = a.astype(BF16)

    def idx_body(c, carry):
        idx_keys(pl.multiple_of(c * CK_IDX, CK_IDX), CK_IDX, False)
        return carry

    assert CK_IDX % TQ == 0 and TQ % KB_IDX == 0
    lax.fori_loop(0, n_idx_full, idx_body, 0)
    for t in range(CK_IDX // TQ - 1):
        @pl.when(q0 - n_idx_full * CK_IDX > t * TQ)
        def _leftover(t=t):
            idx_keys(pl.multiple_of(n_idx_full * CK_IDX + t * TQ, TQ), TQ, False)
    idx_keys(pl.multiple_of(q0, TQ), TQ, True)

    sweep = max(CK_CNT, CK_ATT)
    assert sweep % TQ == 0 and sweep % min(CK_CNT, CK_ATT) == 0

    for t in range(1, sweep // TQ):
        @pl.when((q0 + t * TQ) // sweep == q0 // sweep)
        def _pad_tail(t=t):
            off = pl.multiple_of(q0 + t * TQ, TQ)
            isc_ref[pl.ds(off, TQ), :] = jnp.full((TQ, TQ), -jnp.inf, F32)
            isc16_ref[pl.ds(off, TQ), :] = jnp.full((TQ, TQ), -jnp.inf, BF16)

    n_cnt = q0 // CK_CNT + 1
    needs_search = (q0 + lax.broadcasted_iota(jnp.int32, state, 1) + 1) > TOPK
    thr_ref[...] = jnp.full(state, LOWEST, F32)

    @pl.when(q0 + TQ > TOPK)
    def _search():
        int_min = jnp.int32(-2 ** 31)
        flip = jnp.int32(0x7FFFFFFF)
        high_bits = jnp.int32(-(1 << (32 - COARSE_BITS)))
        half_bucket = 1 << (31 - COARSE_BITS)

        def to_float(u):
            key = u ^ int_min
            return lax.bitcast_convert_type(jnp.where(key >= 0, key, key ^ flip), F32)

        def from_float(f):
            bits = lax.bitcast_convert_type(f, jnp.int32)
            return jnp.where(bits >= 0, bits, bits ^ flip) ^ int_min

        def bucket_value(u):
            bits = lax.bitcast_convert_type(to_float(u), jnp.int32) & high_bits
            return lax.bitcast_convert_type(bits, F32)

        def key_total(x):
            return replicate(jnp.sum(x, axis=0, keepdims=True))

        def count(src_ref, thr_tile, strict=False):
            rows = thr_tile.shape[0]
            one = jnp.ones((), thr_tile.dtype)
            zero = jnp.zeros((), thr_tile.dtype)

            def cnt_body(c, acc):
                off = pl.multiple_of(c * CK_CNT, CK_CNT)
                part = None
                for r in range(CK_CNT // KB_CNT):
                    koff = pl.multiple_of(off + r * KB_CNT, KB_CNT)
                    blk = _tiles(src_ref[pl.ds(koff, KB_CNT), :], rows)
                    hit = blk > thr_tile[None] if strict else blk >= thr_tile[None]
                    ones = _fold(jnp.where(hit, one, zero), jnp.add)
                    part = ones if part is None else part + ones
                return acc + part.astype(F32)

            acc = lax.fori_loop(0, n_cnt, cnt_body, jnp.zeros((rows, TQ), F32))
            return key_total(acc)

        lo_ref[...] = jnp.zeros(state, jnp.int32)

        def coarse_body(step, carry):
            cand = lo_ref[...] | lax.shift_right_logical(int_min, step)
            bucket = bucket_value(cand)
            thr_tile = jnp.concatenate([bucket, bucket], axis=0).astype(BF16)
            cnt = count(isc16_ref, thr_tile)
            lo_ref[...] = jnp.where(cnt >= float(TOPK), cand, lo_ref[...])
            return carry

        lax.fori_loop(0, COARSE_BITS, coarse_body, 0)

        bucket = bucket_value(lo_ref[...])
        exponent = lax.bitcast_convert_type(bucket, jnp.int32) & jnp.int32(0x7F800000)
        zero_tie = jnp.logical_and(needs_search, exponent == 0)
        tie_ref[...] = jnp.where(zero_tie, 1.0, 0.0)
        thr_ref[...] = jnp.where(zero_tie, 0.0, LOWEST)
        done_ref[...] = jnp.where(
            jnp.logical_and(needs_search, jnp.logical_not(zero_tie)), 0.0, 1.0)

        lo_ref[...] = from_float(bucket) - half_bucket
        width_ref[...] = jnp.full(state, 3 * half_bucket, jnp.int32)

        def cond(carry):
            step, active = carry
            return jnp.logical_and(step < COARSE_BITS + 4, active > 0.0)

        def fine_body(carry):
            step, _ = carry
            lo = lo_ref[...]
            width = width_ref[...]
            done = done_ref[...]
            half = lax.shift_right_logical(width, 1)
            mid = lo + half
            thr = to_float(mid)
            cnt = count(isc_ref, thr)
            active = done == 0.0
            ge = cnt >= float(TOPK)
            new_lo = jnp.where(ge, mid, lo)
            new_width = jnp.where(ge, width - half, half)
            hit = jnp.logical_and(active, cnt == float(TOPK))
            closed = jnp.logical_and(
                active, jnp.logical_and(jnp.logical_not(hit), new_width <= 1))
            thr_ref[...] = jnp.where(
                hit, thr, jnp.where(closed, to_float(new_lo), thr_ref[...]))
            tie_ref[...] = jnp.where(closed, 1.0, tie_ref[...])
            lo_ref[...] = jnp.where(active, new_lo, lo)
            width_ref[...] = jnp.where(active, new_width, width)
            done = jnp.where(jnp.logical_or(hit, closed), 1.0, done)
            done_ref[...] = done
            return step + 1, jnp.max(1.0 - done)

        lax.while_loop(cond, fine_body, (jnp.int32(0), jnp.max(1.0 - done_ref[...])))

        @pl.when(jnp.max(tie_ref[...]) > 0.0)
        def _ties():
            thr = thr_ref[...]
            quota = jnp.where(tie_ref[...] > 0.0,
                              float(TOPK) - count(isc_ref, thr, strict=True), 3.0e38)
            sq_r = lax.broadcasted_iota(jnp.int32, (KB_TIE, KB_TIE), 0)
            sq_c = lax.broadcasted_iota(jnp.int32, (KB_TIE, KB_TIE), 1)
            prefix = jnp.where(sq_c <= sq_r, 1.0, 0.0).astype(BF16)

            def tie_body(c, seen):
                off = pl.multiple_of(c * KB_TIE, KB_TIE)
                blk = _tiles(isc_ref[pl.ds(off, KB_TIE), :])
                tied = blk == thr[None]
                tied_f = jnp.where(tied, 1.0, 0.0)
                rank = seen[None] + _tiles(jnp.dot(
                    prefix, tied_f.reshape(KB_TIE, TQ).astype(BF16),
                    preferred_element_type=F32))
                drop = jnp.logical_and(tied, rank > quota[None])
                isc_ref[pl.ds(off, KB_TIE), :] = jnp.where(
                    drop, -jnp.inf, blk).reshape(KB_TIE, TQ)
                return seen + key_total(_fold(tied_f, jnp.add))

            lax.fori_loop(0, n_cnt * (CK_CNT // KB_TIE), tie_body, jnp.zeros(state, F32))

    m_ref[...] = jnp.full(m_ref.shape, NEG_BIG, F32)
    acc_ref[...] = jnp.zeros(acc_ref.shape, F32)
    n_att = q0 // CK_ATT + 1

    def att_body(c, carry):
        off = pl.multiple_of(c * CK_ATT, CK_ATT)
        thr = thr_ref[...]
        blocks = [slice(r * KB_ATT, (r + 1) * KB_ATT) for r in range(CK_ATT // KB_ATT)]
        for rs in blocks:
            rows = pl.ds(pl.multiple_of(off + rs.start, KB_ATT), KB_ATT)
            bias_ref[rs, :] = jnp.where(_tiles(isc_ref[rows, :]) >= thr[None],
                                        0.0, NEG_BIG).reshape(KB_ATT, TQ)
        for h in range(N_HEADS):
            j = h // 2
            tile_max = None
            for rs in blocks:
                rows = pl.ds(pl.multiple_of(off + rs.start, KB_ATT), KB_ATT)
                kk = k_ref[0, rows, j * LANES:(j + 1) * LANES]
                s = jnp.dot(kk, qm_ref[h], preferred_element_type=F32) + bias_ref[rs, :]
                s_ref[h, rs, :] = s
                mx = _fold(_tiles(s), jnp.maximum)
                tile_max = mx if tile_max is None else jnp.maximum(tile_max, mx)
            m_old = m_ref[h]
            m_new = jnp.maximum(m_old, replicate(jnp.max(tile_max, axis=0, keepdims=True)))
            alpha_ref[h] = jnp.exp2(m_old - m_new)
            m_ref[h] = m_new
        for h in range(N_HEADS):
            m_new = m_ref[h]
            for rs in blocks:
                d = (_tiles(s_ref[h, rs, :]) - m_new[None]).reshape(KB_ATT, TQ)
                p_ref[h, rs, :] = jnp.exp2(d.astype(BF16))
        for h in range(N_HEADS):
            vv = vT_ref[0, h, :, pl.ds(off, CK_ATT)]
            pv = jnp.dot(vv, p_ref[h], preferred_element_type=F32)
            acc_ref[h] = alpha_ref[h][0:1, :] * acc_ref[h] + pv
        return carry

    lax.fori_loop(0, n_att, att_body, 0)

    def normalized(h):
        acc = acc_ref[h]
        return acc[0:HEAD_DIM, :] / acc[HEAD_DIM:HEAD_DIM + 1, :]

    for j in range(N_PAIRS):
        out_t = jnp.concatenate([normalized(2 * j), normalized(2 * j + 1)], axis=0)
        o_ref[0, :, j * LANES:(j + 1) * LANES] = out_t.T


def _attn_call(qT, qiT, wT, k, vT, ki):
    b, s, _ = k.shape
    tok = lambda bi, ti: (bi, ti, 0)
    tok_t = lambda bi, ti: (bi, 0, ti)
    per_batch = lambda bi, ti: (bi, 0, 0)
    resident = functools.partial(pl.BlockSpec, pipeline_mode=pl.Buffered(1))
    state = pltpu.VMEM((SUBLANES, TQ), F32)
    state_i = pltpu.VMEM((SUBLANES, TQ), jnp.int32)
    per_head = pltpu.VMEM((N_HEADS, SUBLANES, TQ), F32)
    return pl.pallas_call(
        _attn_kernel,
        grid=(b, s // TQ),
        in_specs=[pl.BlockSpec((1, WIDTH, TQ), tok_t),
                  pl.BlockSpec((1, WIDTH, TQ), tok_t),
                  pl.BlockSpec((1, N_HEADS, TQ), tok_t),
                  resident((1, s, WIDTH), per_batch),
                  resident((1, N_HEADS, V_ROWS, s), lambda bi, ti: (bi, 0, 0, 0)),
                  resident((1, s, LANES), per_batch)],
        out_specs=pl.BlockSpec((1, TQ, WIDTH), tok),
        out_shape=jax.ShapeDtypeStruct((b, s, WIDTH), F32),
        scratch_shapes=[pltpu.VMEM((s, TQ), F32),
                        pltpu.VMEM((s, TQ), BF16),
                        pltpu.VMEM((N_HEADS, LANES, TQ), BF16),
                        pltpu.VMEM((N_HEADS, LANES, TQ), BF16),
                        state,
                        state_i,
                        state_i,
                        state,
                        state,
                        pltpu.VMEM((CK_ATT, TQ), F32),
                        pltpu.VMEM((N_HEADS, CK_ATT, TQ), F32),
                        pltpu.VMEM((N_HEADS, CK_ATT, TQ), BF16),
                        per_head,
                        per_head,
                        pltpu.VMEM((N_HEADS, V_ROWS, TQ), F32)],
        compiler_params=pltpu.CompilerParams(
            dimension_semantics=("parallel", "arbitrary"), vmem_limit_bytes=VMEM_LIMIT),
        name="dsa_attention",
    )(qT, qiT, wT, k, vT, ki)


def _out_kernel(x_ref, a_ref, gmix_ref, g1_ref, sc2_ref, sh2_ref, g2_ref,
                ba_ref, bg_ref, n2_ref, fg_ref, wo_ref, w1_ref, w2_ref, o_ref,
                *, apply_final):
    def rms(t, g):
        return t * lax.rsqrt(jnp.mean(t * t, axis=-1, keepdims=True) + EPS) * g

    x = x_ref[0]
    a = rms(a_ref[0], ba_ref[...]).astype(BF16)
    g = rms(gmix_ref[0], bg_ref[...]).astype(BF16)
    y = (jnp.dot(a, wo_ref[0:WIDTH, :], preferred_element_type=F32)
         + jnp.dot(g, wo_ref[WIDTH:2 * WIDTH, :], preferred_element_type=F32))
    x1 = x + g1_ref[0] * y
    h2 = (rms(x1, n2_ref[...]) * (1.0 + sc2_ref[0]) + sh2_ref[0]).astype(BF16)
    ff = None
    d_ff = w1_ref.shape[1]
    for f in range(d_ff // FF_CHUNK):
        fs = slice(f * FF_CHUNK, (f + 1) * FF_CHUNK)
        u = jnp.maximum(jnp.dot(h2, w1_ref[:, fs], preferred_element_type=F32), 0.0)
        part = jnp.dot((u * u).astype(BF16), w2_ref[fs, :], preferred_element_type=F32)
        ff = part if ff is None else ff + part
    x2 = x1 + g2_ref[0] * ff
    o_ref[0] = rms(x2, fg_ref[...]) if apply_final else x2


def _out_call(x, attn, gmix, g1, sc2, sh2, g2, ba, bg, n2, fg, wo, w1, w2, apply_final):
    b, s, d = x.shape
    tm = TM_OUT
    tok = lambda bi, ti: (bi, ti, 0)
    per_b = lambda bi, ti: (bi, 0, 0)
    const2 = lambda bi, ti: (0, 0)
    resident = functools.partial(pl.BlockSpec, pipeline_mode=pl.Buffered(1))
    vec_b = pl.BlockSpec((1, 1, d), per_b)
    return pl.pallas_call(
        functools.partial(_out_kernel, apply_final=apply_final),
        grid=(b, s // tm),
        in_specs=[pl.BlockSpec((1, tm, d), tok),
                  pl.BlockSpec((1, tm, WIDTH), tok),
                  pl.BlockSpec((1, tm, WIDTH), tok),
                  vec_b, vec_b, vec_b, vec_b,
                  pl.BlockSpec((1, WIDTH), const2),
                  pl.BlockSpec((1, WIDTH), const2),
                  pl.BlockSpec((1, d), const2),
                  pl.BlockSpec((1, d), const2),
                  resident(wo.shape, const2),
                  resident(w1.shape, const2),
                  resident(w2.shape, const2)],
        out_specs=pl.BlockSpec((1, tm, d), tok),
        out_shape=jax.ShapeDtypeStruct((b, s, d), F32),
        compiler_params=pltpu.CompilerParams(
            dimension_semantics=("parallel", "parallel"), vmem_limit_bytes=VMEM_LIMIT),
        name="outproj_mlp",
    )(x, attn, gmix, g1, sc2, sh2, g2, ba, bg, n2, fg, wo, w1, w2)


def _rope_tables(s):
    half = HEAD_DIM // 2
    inv_freq = ROPE_THETA ** (-jnp.arange(half, dtype=F32) / half)
    ang = jnp.arange(s).astype(F32)[:, None] * inv_freq[None, :]
    cos = jnp.cos(ang)
    sin = jnp.sin(ang)
    cos_t = jnp.tile(jnp.concatenate([cos, cos], axis=-1), (1, LANES // HEAD_DIM))
    sin_t = jnp.tile(jnp.concatenate([-sin, sin], axis=-1), (1, LANES // HEAD_DIM))
    return cos_t, sin_t


def kernel(x, c, ada_w, ada_b, norm1_g, w_in, w_spatial, b_spatial, gm_norm_g,
           beta_attn, beta_gmlp, w_out, norm2_g, w_ff1, w_ff2, final_g):
    b, s, d = x.shape
    depth = ada_w.shape[0]
    assert d == 2 * WIDTH and w_in.shape[2] == 6 * WIDTH + HEAD_DIM + N_HEADS
    cos_t, sin_t = _rope_tables(s)
    c_pad = jnp.zeros((8, d), F32).at[:b].set(c)
    kw0 = 4 * WIDTH
    kw1 = kw0 + HEAD_DIM + N_HEADS
    for l in range(depth):
        mod = _mod_call(c_pad, ada_w, ada_b[:, None, :], l)[:b]
        sh1, sc1, g1, sh2, sc2, g2 = [m[:, None, :] for m in jnp.split(mod, 6, axis=-1)]
        w_main = jnp.concatenate([w_in[l][:, :kw0], w_in[l][:, kw1:]], axis=1).astype(BF16)
        w_kw = jnp.pad(w_in[l][:, kw0:kw1], ((0, 0), (0, LANES - (kw1 - kw0)))).astype(BF16)
        bsp = jnp.repeat(b_spatial[l].reshape(N_PAIRS, 2, CHUNK).transpose(0, 2, 1),
                         HEAD_DIM, axis=2)
        qT, k, vT, qiT, ki, wT, gmix = _inproj_call(
            x, sc1, sh1, norm1_g[l][None, :], w_main, w_kw, cos_t, sin_t,
            w_spatial[l], bsp, gm_norm_g[l][None, :])
        attn = _attn_call(qT, qiT, wT, k, vT, ki)
        x = _out_call(x, attn, gmix, g1, sc2, sh2, g2,
                      beta_attn[l][None, :], beta_gmlp[l][None, :], norm2_g[l][None, :],
                      final_g[None, :], w_out[l].astype(BF16), w_ff1[l].astype(BF16),
                      w_ff2[l].astype(BF16), apply_final=(l == depth - 1))
    return x
```

```python
import functools

import jax
import jax.numpy as jnp
from jax import lax
from jax.experimental import pallas as pl
from jax.experimental.pallas import tpu as pltpu

F32 = jnp.float32
BF16 = jnp.bfloat16

LANES = 128
HEAD_DIM = 64
N_HEADS = 8
N_PAIRS = N_HEADS // 2
WIDTH = N_HEADS * HEAD_DIM
CHUNK = 128
TOPK = 256
ROPE_THETA = 10000.0
EPS = 1e-6
LOG2E = 1.4426950408889634
NEG_BIG = -1e30
LOWEST = -3.0e38
VMEM_LIMIT = 56 * 1024 * 1024

TM_IN = 512
TQ = 256
SUBLANES = 8
KB_IDX = 128
CK_IDX = 512
KB_TIE = 512
COARSE_BITS = 16
FINE_BLIND = 8
CK_CNT = 512
KB_CNT = 128
CK_ATT = 512
KB_ATT = 128
V_ROWS = 80
TM_OUT = 512
FF_CHUNK = 1024


def _gelu_tanh(x):
    return 0.5 * x * (1.0 + jnp.tanh(0.7978845608028654 * (x + 0.044715 * x * x * x)))


def _mod_kernel(c_ref, w_ref, b_ref, o_ref):
    c = c_ref[...]
    ca = c / (1.0 + jnp.exp(-c))
    w = w_ref[0]
    ca_hi = ca.astype(BF16)
    ca_lo = (ca - ca_hi.astype(F32)).astype(BF16)
    w_hi = w.astype(BF16)
    w_lo = (w - w_hi.astype(F32)).astype(BF16)
    dot = functools.partial(jnp.dot, preferred_element_type=F32)
    o_ref[...] = dot(ca_hi, w_hi) + (dot(ca_hi, w_lo) + dot(ca_lo, w_hi)) + b_ref[0]


def _mod_call(c_pad, w, b, layer):
    rows, d = c_pad.shape
    n = w.shape[2]
    tn = 1024
    return pl.pallas_call(
        _mod_kernel,
        grid=(n // tn,),
        in_specs=[pl.BlockSpec((rows, d), lambda j: (0, 0)),
                  pl.BlockSpec((1, d, tn), lambda j: (layer, 0, j)),
                  pl.BlockSpec((1, 1, tn), lambda j: (layer, 0, j))],
        out_specs=pl.BlockSpec((rows, tn), lambda j: (0, j)),
        out_shape=jax.ShapeDtypeStruct((rows, n), F32),
        name="adaln_mod",
    )(c_pad, w, b)


def _inproj_kernel(x_ref, sc_ref, sh_ref, g_ref, wm_ref, wk_ref, cos_ref, sin_ref,
                   wsp_ref, bsp_ref, gmg_ref,
                   qT_ref, k_ref, vT_ref, qiT_ref, ki_ref, wT_ref, gm_ref):
    tm = x_ref.shape[1]
    x = x_ref[0]
    ms = jnp.mean(x * x, axis=-1, keepdims=True)
    h = x * lax.rsqrt(ms + EPS) * g_ref[...]
    h = h * (1.0 + sc_ref[0]) + sh_ref[0]
    hb = h.astype(BF16)

    cos = cos_ref[...]
    sin = sin_ref[...]
    lane = lax.broadcasted_iota(jnp.int32, (tm, LANES), 1)
    first_half = (lane & (HEAD_DIM // 2)) == 0
    low_head = lane < HEAD_DIM

    def rope(t):
        partner = jnp.where(first_half, pltpu.roll(t, LANES - HEAD_DIM // 2, 1),
                            pltpu.roll(t, HEAD_DIM // 2, 1))
        return t * cos + partner * sin

    def proj(col, width=WIDTH):
        return jnp.dot(hb, wm_ref[:, col:col + width], preferred_element_type=F32)

    pq = proj(0)
    for j in range(N_PAIRS):
        sl = slice(j * LANES, (j + 1) * LANES)
        qT_ref[0, sl, :] = (rope(pq[:, sl]) * (HEAD_DIM ** -0.5 * LOG2E)).T.astype(BF16)
    pk = proj(WIDTH)
    for j in range(N_PAIRS):
        sl = slice(j * LANES, (j + 1) * LANES)
        k_ref[0, :, sl] = rope(pk[:, sl]).astype(BF16)
    pv = proj(2 * WIDTH)
    ones_row = jnp.where(lane == HEAD_DIM, 1.0, 0.0)
    for j in range(N_PAIRS):
        pair = pv[:, j * LANES:(j + 1) * LANES]
        vT_ref[0, 2 * j] = jnp.where(low_head, pair, ones_row).T[0:V_ROWS].astype(BF16)
        vT_ref[0, 2 * j + 1] = jnp.where(
            low_head, pltpu.roll(pair, HEAD_DIM, 1), ones_row).T[0:V_ROWS].astype(BF16)
    pqi = proj(3 * WIDTH)
    for j in range(N_PAIRS):
        sl = slice(j * LANES, (j + 1) * LANES)
        qiT_ref[0, sl, :] = (rope(pqi[:, sl]) * (HEAD_DIM ** -0.5)).T.astype(BF16)

    pkw = jnp.dot(hb, wk_ref[...], preferred_element_type=F32)
    rk = rope(pkw)
    ki_ref[0] = jnp.where(low_head, rk, pltpu.roll(rk, HEAD_DIM, 1)).astype(BF16)
    wT_ref[0] = pkw.T[HEAD_DIM:HEAD_DIM + N_HEADS, :] * (N_HEADS ** -0.5)

    gu = proj(4 * WIDTH)
    gv = proj(5 * WIDTH)
    lane_c = lax.broadcasted_iota(jnp.int32, (CHUNK, LANES), 1)
    row_c = lax.broadcasted_iota(jnp.int32, (CHUNK, LANES), 0)
    low_c = lane_c < HEAD_DIM
    causal = lane_c <= row_c
    inv_n = 1.0 / HEAD_DIM
    for j in range(N_PAIRS):
        sl = slice(j * LANES, (j + 1) * LANES)
        w_lo = jnp.where(causal, wsp_ref[2 * j], 0.0).astype(BF16)
        w_hi = jnp.where(causal, wsp_ref[2 * j + 1], 0.0).astype(BF16)
        gain = gmg_ref[:, sl]
        bias = bsp_ref[j]
        for cidx in range(tm // CHUNK):
            rs = slice(cidx * CHUNK, (cidx + 1) * CHUNK)
            vv = _gelu_tanh(gv[rs, sl])
            s_all = jnp.sum(vv, axis=-1, keepdims=True)
            s_lo = jnp.sum(jnp.where(low_c, vv, 0.0), axis=-1, keepdims=True)
            mu = jnp.where(low_c, s_lo, s_all - s_lo) * inv_n
            dv = vv - mu
            d2 = dv * dv
            q_all = jnp.sum(d2, axis=-1, keepdims=True)
            q_lo = jnp.sum(jnp.where(low_c, d2, 0.0), axis=-1, keepdims=True)
            var = jnp.where(low_c, q_lo, q_all - q_lo) * inv_n
            vn = (dv * lax.rsqrt(var + EPS) * gain).astype(BF16)
            m_lo = jnp.dot(w_lo, vn, preferred_element_type=F32)
            m_hi = jnp.dot(w_hi, vn, preferred_element_type=F32)
            mixed = jnp.where(low_c, m_lo, m_hi) + bias
            gm_ref[0, rs, sl] = _gelu_tanh(gu[rs, sl]) * mixed


def _inproj_call(x, sc1, sh1, g1n, w_main, w_kw, cos_t, sin_t, wsp, bsp, gmg):
    b, s, d = x.shape
    tm = TM_IN
    const2 = lambda bi, ti: (0, 0)
    const3 = lambda bi, ti: (0, 0, 0)
    tok = lambda bi, ti: (bi, ti, 0)
    tok_t = lambda bi, ti: (bi, 0, ti)
    return pl.pallas_call(
        _inproj_kernel,
        grid=(b, s // tm),
        in_specs=[pl.BlockSpec((1, tm, d), tok),
                  pl.BlockSpec((1, 1, d), lambda bi, ti: (bi, 0, 0)),
                  pl.BlockSpec((1, 1, d), lambda bi, ti: (bi, 0, 0)),
                  pl.BlockSpec((1, d), const2),
                  pl.BlockSpec(w_main.shape, const2),
                  pl.BlockSpec(w_kw.shape, const2),
                  pl.BlockSpec((tm, LANES), lambda bi, ti: (ti, 0)),
                  pl.BlockSpec((tm, LANES), lambda bi, ti: (ti, 0)),
                  pl.BlockSpec(wsp.shape, const3),
                  pl.BlockSpec(bsp.shape, const3),
                  pl.BlockSpec(gmg.shape, const2)],
        out_specs=[pl.BlockSpec((1, WIDTH, tm), tok_t),
                   pl.BlockSpec((1, tm, WIDTH), tok),
                   pl.BlockSpec((1, N_HEADS, V_ROWS, tm), lambda bi, ti: (bi, 0, 0, ti)),
                   pl.BlockSpec((1, WIDTH, tm), tok_t),
                   pl.BlockSpec((1, tm, LANES), tok),
                   pl.BlockSpec((1, N_HEADS, tm), tok_t),
                   pl.BlockSpec((1, tm, WIDTH), tok)],
        out_shape=[jax.ShapeDtypeStruct((b, WIDTH, s), BF16),
                   jax.ShapeDtypeStruct((b, s, WIDTH), BF16),
                   jax.ShapeDtypeStruct((b, N_HEADS, V_ROWS, s), BF16),
                   jax.ShapeDtypeStruct((b, WIDTH, s), BF16),
                   jax.ShapeDtypeStruct((b, s, LANES), BF16),
                   jax.ShapeDtypeStruct((b, N_HEADS, s), F32),
                   jax.ShapeDtypeStruct((b, s, WIDTH), F32)],
        compiler_params=pltpu.CompilerParams(
            dimension_semantics=("parallel", "parallel"), vmem_limit_bytes=VMEM_LIMIT),
        name="inproj_gmlp",
    )(x, sc1, sh1, g1n, w_main, w_kw, cos_t, sin_t, wsp, bsp, gmg)


def _tiles(x, rows=SUBLANES):
    return x.reshape(x.shape[0] // rows, rows, x.shape[1])


def _fold(x3, op, chains=4):
    accs = [x3[i] for i in range(min(chains, x3.shape[0]))]
    for i in range(len(accs), x3.shape[0]):
        accs[i % chains] = op(accs[i % chains], x3[i])
    while len(accs) > 1:
        accs = [op(accs[i], accs[i + 1]) if i + 1 < len(accs) else accs[i]
                for i in range(0, len(accs), 2)]
    return accs[0]


def _attn_kernel(qT_ref, qiT_ref, wT_ref, k_ref, vT_ref, ki_ref, o_ref,
                 isc_ref, isc16_ref, qm_ref, qim_ref,
                 thr_ref, lo_ref, width_ref, done_ref, tie_ref,
                 bias_ref, s_ref, p_ref, alpha_ref, m_ref, acc_ref):
    i = pl.program_id(1)
    q0 = i * TQ
    state = (SUBLANES, TQ)
    top_half = lax.broadcasted_iota(jnp.int32, (LANES, TQ), 0) < HEAD_DIM

    def replicate(row):
        return jnp.broadcast_to(row, state)

    qT = qT_ref[0].astype(F32)
    qiT = qiT_ref[0].astype(F32)
    for h in range(N_HEADS):
        sl = slice((h // 2) * LANES, (h // 2 + 1) * LANES)
        keep = top_half if h % 2 == 0 else jnp.logical_not(top_half)
        qm_ref[h] = jnp.where(keep, qT[sl, :], 0.0).astype(BF16)
        qim_ref[h] = jnp.where(keep, qiT[sl, :], 0.0).astype(BF16)

    n_idx_full = q0 // CK_IDX
    key_iota = lax.broadcasted_iota(jnp.int32, (KB_IDX, TQ), 0)
    qry_iota = lax.broadcasted_iota(jnp.int32, (KB_IDX, TQ), 1)

    def idx_keys(off, nkeys, masked):
        for r in range(nkeys // KB_IDX):
            koff = pl.multiple_of(off + r * KB_IDX, KB_IDX)
            kic = ki_ref[0, pl.ds(koff, KB_IDX), :]
            a = None
            for h in range(N_HEADS):
                lg = jnp.dot(kic, qim_ref[h], preferred_element_type=F32)
                t = jnp.maximum(lg, 0.0) * wT_ref[0, h:h + 1, :]
                a = t if a is None else a + t
            if masked:
                a = jnp.where(koff + key_iota <= q0 + qry_iota, a, -jnp.inf)
            isc_ref[pl.ds(koff, KB_IDX), :] = a
            isc16_ref[pl.ds(koff, KB_IDX), :] = a.astype(BF16)

    def idx_body(c, carry):
        idx_keys(pl.multiple_of(c * CK_IDX, CK_IDX), CK_IDX, False)
        return carry

    assert CK_IDX % TQ == 0 and TQ % KB_IDX == 0
    lax.fori_loop(0, n_idx_full, idx_body, 0)
    for t in range(CK_IDX // TQ - 1):
        @pl.when(q0 - n_idx_full * CK_IDX > t * TQ)
        def _leftover(t=t):
            idx_keys(pl.multiple_of(n_idx_full * CK_IDX + t * TQ, TQ), TQ, False)
    idx_keys(pl.multiple_of(q0, TQ), TQ, True)

    sweep = max(CK_CNT, CK_ATT)
    assert sweep % TQ == 0 and sweep % min(CK_CNT, CK_ATT) == 0

    for t in range(1, sweep // TQ):
        @pl.when((q0 + t * TQ) // sweep == q0 // sweep)
        def _pad_tail(t=t):
            off = pl.multiple_of(q0 + t * TQ, TQ)
            isc_ref[pl.ds(off, TQ), :] = jnp.full((TQ, TQ), -jnp.inf, F32)
            isc16_ref[pl.ds(off, TQ), :] = jnp.full((TQ, TQ), -jnp.inf, BF16)

    n_cnt = q0 // CK_CNT + 1
    needs_search = (q0 + lax.broadcasted_iota(jnp.int32, state, 1) + 1) > TOPK
    thr_ref[...] = jnp.full(state, LOWEST, F32)

    @pl.when(q0 + TQ > TOPK)
    def _search():
        int_min = jnp.int32(-2 ** 31)
        flip = jnp.int32(0x7FFFFFFF)
        high_bits = jnp.int32(-(1 << (32 - COARSE_BITS)))
        half_bucket = 1 << (31 - COARSE_BITS)

        def to_float(u):
            key = u ^ int_min
            return lax.bitcast_convert_type(jnp.where(key >= 0, key, key ^ flip), F32)

        def from_float(f):
            bits = lax.bitcast_convert_type(f, jnp.int32)
            return jnp.where(bits >= 0, bits, bits ^ flip) ^ int_min

        def bucket_value(u):
            bits = lax.bitcast_convert_type(to_float(u), jnp.int32) & high_bits
            return lax.bitcast_convert_type(bits, F32)

        def key_total(x):
            return replicate(jnp.sum(x, axis=0, keepdims=True))

        def count(src_ref, thr_tile, strict=False):
            rows = thr_tile.shape[0]
            one = jnp.ones((), thr_tile.dtype)
            zero = jnp.zeros((), thr_tile.dtype)

            def cnt_body(c, acc):
                off = pl.multiple_of(c * CK_CNT, CK_CNT)
                part = None
                for r in range(CK_CNT // KB_CNT):
                    koff = pl.multiple_of(off + r * KB_CNT, KB_CNT)
                    blk = _tiles(src_ref[pl.ds(koff, KB_CNT), :], rows)
                    hit = blk > thr_tile[None] if strict else blk >= thr_tile[None]
                    ones = _fold(jnp.where(hit, one, zero), jnp.add)
                    part = ones if part is None else part + ones
                return acc + part.astype(F32)

            acc = lax.fori_loop(0, n_cnt, cnt_body, jnp.zeros((rows, TQ), F32))
            return key_total(acc)

        lo_ref[...] = jnp.zeros(state, jnp.int32)

        def coarse_body(step, carry):
            cand = lo_ref[...] | lax.shift_right_logical(int_min, step)
            bucket = bucket_value(cand)
            thr_tile = jnp.concatenate([bucket, bucket], axis=0).astype(BF16)
            cnt = count(isc16_ref, thr_tile)
            lo_ref[...] = jnp.where(cnt >= float(TOPK), cand, lo_ref[...])
            return carry

        lax.fori_loop(0, COARSE_BITS, coarse_body, 0)

        bucket = bucket_value(lo_ref[...])
        exponent = lax.bitcast_convert_type(bucket, jnp.int32) & jnp.int32(0x7F800000)
        zero_tie = jnp.logical_and(needs_search, exponent == 0)
        tie_ref[...] = jnp.where(zero_tie, 1.0, 0.0)
        thr_ref[...] = jnp.where(zero_tie, 0.0, LOWEST)
        done_ref[...] = jnp.where(
            jnp.logical_and(needs_search, jnp.logical_not(zero_tie)), 0.0, 1.0)

        lo_ref[...] = from_float(bucket) - half_bucket
        width_ref[...] = jnp.full(state, 3 * half_bucket, jnp.int32)

        def cond(carry):
            step, active = carry
            return jnp.logical_and(step < COARSE_BITS + 4, active > 0.0)

        def fine_step():
            lo = lo_ref[...]
            width = width_ref[...]
            done = done_ref[...]
            half = lax.shift_right_logical(width, 1)
            mid = lo + half
            thr = to_float(mid)
            cnt = count(isc_ref, thr)
            active = done == 0.0
            ge = cnt >= float(TOPK)
            new_lo = jnp.where(ge, mid, lo)
            new_width = jnp.where(ge, width - half, half)
            hit = jnp.logical_and(active, cnt == float(TOPK))
            closed = jnp.logical_and(
                active, jnp.logical_and(jnp.logical_not(hit), new_width <= 1))
            thr_ref[...] = jnp.where(
                hit, thr, jnp.where(closed, to_float(new_lo), thr_ref[...]))
            tie_ref[...] = jnp.where(closed, 1.0, tie_ref[...])
            lo_ref[...] = jnp.where(active, new_lo, lo)
            width_ref[...] = jnp.where(active, new_width, width)
            done = jnp.where(jnp.logical_or(hit, closed), 1.0, done)
            done_ref[...] = done
            return done

        def blind_body(step, carry):
            fine_step()
            return carry

        def fine_body(carry):
            step, _ = carry
            return step + 1, jnp.max(1.0 - fine_step())

        lax.fori_loop(0, FINE_BLIND, blind_body, 0)
        lax.while_loop(cond, fine_body,
                       (jnp.int32(FINE_BLIND), jnp.max(1.0 - done_ref[...])))

        @pl.when(jnp.max(tie_ref[...]) > 0.0)
        def _ties():
            thr = thr_ref[...]
            quota = jnp.where(tie_ref[...] > 0.0,
                              float(TOPK) - count(isc_ref, thr, strict=True), 3.0e38)
            sq_r = lax.broadcasted_iota(jnp.int32, (KB_TIE, KB_TIE), 0)
            sq_c = lax.broadcasted_iota(jnp.int32, (KB_TIE, KB_TIE), 1)
            prefix = jnp.where(sq_c <= sq_r, 1.0, 0.0).astype(BF16)

            def tie_body(c, seen):
                off = pl.multiple_of(c * KB_TIE, KB_TIE)
                blk = _tiles(isc_ref[pl.ds(off, KB_TIE), :])
                tied = blk == thr[None]
                tied_f = jnp.where(tied, 1.0, 0.0)
                rank = seen[None] + _tiles(jnp.dot(
                    prefix, tied_f.reshape(KB_TIE, TQ).astype(BF16),
                    preferred_element_type=F32))
                drop = jnp.logical_and(tied, rank > quota[None])
                isc_ref[pl.ds(off, KB_TIE), :] = jnp.where(
                    drop, -jnp.inf, blk).reshape(KB_TIE, TQ)
                return seen + key_total(_fold(tied_f, jnp.add))

            lax.fori_loop(0, n_cnt * (CK_CNT // KB_TIE), tie_body, jnp.zeros(state, F32))

    m_ref[...] = jnp.full(m_ref.shape, NEG_BIG, F32)
    acc_ref[...] = jnp.zeros(acc_ref.shape, F32)
    n_att = q0 // CK_ATT + 1

    def att_body(c, carry):
        off = pl.multiple_of(c * CK_ATT, CK_ATT)
        thr = thr_ref[...]
        blocks = [slice(r * KB_ATT, (r + 1) * KB_ATT) for r in range(CK_ATT // KB_ATT)]
        for rs in blocks:
            rows = pl.ds(pl.multiple_of(off + rs.start, KB_ATT), KB_ATT)
            bias_ref[rs, :] = jnp.where(_tiles(isc_ref[rows, :]) >= thr[None],
                                        0.0, NEG_BIG).reshape(KB_ATT, TQ)
        for h in range(N_HEADS):
            j = h // 2
            tile_max = None
            for rs in blocks:
                rows = pl.ds(pl.multiple_of(off + rs.start, KB_ATT), KB_ATT)
                kk = k_ref[0, rows, j * LANES:(j + 1) * LANES]
                s = jnp.dot(kk, qm_ref[h], preferred_element_type=F32) + bias_ref[rs, :]
                s_ref[h, rs, :] = s
                mx = _fold(_tiles(s), jnp.maximum)
                tile_max = mx if tile_max is None else jnp.maximum(tile_max, mx)
            m_old = m_ref[h]
            m_new = jnp.maximum(m_old, replicate(jnp.max(tile_max, axis=0, keepdims=True)))
            alpha_ref[h] = jnp.exp2(m_old - m_new)
            m_ref[h] = m_new
        for h in range(N_HEADS):
            m_new = m_ref[h]
            for rs in blocks:
                d = (_tiles(s_ref[h, rs, :]) - m_new[None]).reshape(KB_ATT, TQ)
                p_ref[h, rs, :] = jnp.exp2(d.astype(BF16))
        for h in range(N_HEADS):
            vv = vT_ref[0, h, :, pl.ds(off, CK_ATT)]
            pv = jnp.dot(vv, p_ref[h], preferred_element_type=F32)
            acc_ref[h] = alpha_ref[h][0:1, :] * acc_ref[h] + pv
        return carry

    lax.fori_loop(0, n_att, att_body, 0)

    def normalized(h):
        acc = acc_ref[h]
        return acc[0:HEAD_DIM, :] / acc[HEAD_DIM:HEAD_DIM + 1, :]

    for j in range(N_PAIRS):
        out_t = jnp.concatenate([normalized(2 * j), normalized(2 * j + 1)], axis=0)
        o_ref[0, :, j * LANES:(j + 1) * LANES] = out_t.T


def _attn_call(qT, qiT, wT, k, vT, ki):
    b, s, _ = k.shape
    tok = lambda bi, ti: (bi, ti, 0)
    tok_t = lambda bi, ti: (bi, 0, ti)
    per_batch = lambda bi, ti: (bi, 0, 0)
    resident = functools.partial(pl.BlockSpec, pipeline_mode=pl.Buffered(1))
    state = pltpu.VMEM((SUBLANES, TQ), F32)
    state_i = pltpu.VMEM((SUBLANES, TQ), jnp.int32)
    per_head = pltpu.VMEM((N_HEADS, SUBLANES, TQ), F32)
    return pl.pallas_call(
        _attn_kernel,
        grid=(b, s // TQ),
        in_specs=[pl.BlockSpec((1, WIDTH, TQ), tok_t),
                  pl.BlockSpec((1, WIDTH, TQ), tok_t),
                  pl.BlockSpec((1, N_HEADS, TQ), tok_t),
                  resident((1, s, WIDTH), per_batch),
                  resident((1, N_HEADS, V_ROWS, s), lambda bi, ti: (bi, 0, 0, 0)),
                  resident((1, s, LANES), per_batch)],
        out_specs=pl.BlockSpec((1, TQ, WIDTH), tok),
        out_shape=jax.ShapeDtypeStruct((b, s, WIDTH), F32),
        scratch_shapes=[pltpu.VMEM((s, TQ), F32),
                        pltpu.VMEM((s, TQ), BF16),
                        pltpu.VMEM((N_HEADS, LANES, TQ), BF16),
                        pltpu.VMEM((N_HEADS, LANES, TQ), BF16),
                        state,
                        state_i,
                        state_i,
                        state,
                        state,
                        pltpu.VMEM((CK_ATT, TQ), F32),
                        pltpu.VMEM((N_HEADS, CK_ATT, TQ), F32),
                        pltpu.VMEM((N_HEADS, CK_ATT, TQ), BF16),
                        per_head,
                        per_head,
                        pltpu.VMEM((N_HEADS, V_ROWS, TQ), F32)],
        compiler_params=pltpu.CompilerParams(
            dimension_semantics=("parallel", "arbitrary"), vmem_limit_bytes=VMEM_LIMIT),
        name="dsa_attention",
    )(qT, qiT, wT, k, vT, ki)


def _out_kernel(x_ref, a_ref, gmix_ref, g1_ref, sc2_ref, sh2_ref, g2_ref,
                ba_ref, bg_ref, n2_ref, fg_ref, wo_ref, w1_ref, w2_ref, o_ref,
                *, apply_final):
    def rms(t, g):
        return t * lax.rsqrt(jnp.mean(t * t, axis=-1, keepdims=True) + EPS) * g

    x = x_ref[0]
    a = rms(a_ref[0], ba_ref[...]).astype(BF16)
    g = rms(gmix_ref[0], bg_ref[...]).astype(BF16)
    y = (jnp.dot(a, wo_ref[0:WIDTH, :], preferred_element_type=F32)
         + jnp.dot(g, wo_ref[WIDTH:2 * WIDTH, :], preferred_element_type=F32))
    x1 = x + g1_ref[0] * y
    h2 = (rms(x1, n2_ref[...]) * (1.0 + sc2_ref[0]) + sh2_ref[0]).astype(BF16)
    ff = None
    d_ff = w1_ref.shape[1]
    for f in range(d_ff // FF_CHUNK):
        fs = slice(f * FF_CHUNK, (f + 1) * FF_CHUNK)
        u = jnp.maximum(jnp.dot(h2, w1_ref[:, fs], preferred_element_type=F32), 0.0)
        part = jnp.dot((u * u).astype(BF16), w2_ref[fs, :], preferred_element_type=F32)
        ff = part if ff is None else ff + part
    x2 = x1 + g2_ref[0] * ff
    o_ref[0] = rms(x2, fg_ref[...]) if apply_final else x2


def _out_call(x, attn, gmix, g1, sc2, sh2, g2, ba, bg, n2, fg, wo, w1, w2, apply_final):
    b, s, d = x.shape
    tm = TM_OUT
    tok = lambda bi, ti: (bi, ti, 0)
    per_b = lambda bi, ti: (bi, 0, 0)
    const2 = lambda bi, ti: (0, 0)
    resident = functools.partial(pl.BlockSpec, pipeline_mode=pl.Buffered(1))
    vec_b = pl.BlockSpec((1, 1, d), per_b)
    return pl.pallas_call(
        functools.partial(_out_kernel, apply_final=apply_final),
        grid=(b, s // tm),
        in_specs=[pl.BlockSpec((1, tm, d), tok),
                  pl.BlockSpec((1, tm, WIDTH), tok),
                  pl.BlockSpec((1, tm, WIDTH), tok),
                  vec_b, vec_b, vec_b, vec_b,
                  pl.BlockSpec((1, WIDTH), const2),
                  pl.BlockSpec((1, WIDTH), const2),
                  pl.BlockSpec((1, d), const2),
                  pl.BlockSpec((1, d), const2),
                  resident(wo.shape, const2),
                  resident(w1.shape, const2),
                  resident(w2.shape, const2)],
        out_specs=pl.BlockSpec((1, tm, d), tok),
        out_shape=jax.ShapeDtypeStruct((b, s, d), F32),
        compiler_params=pltpu.CompilerParams(
            dimension_semantics=("parallel", "parallel"), vmem_limit_bytes=VMEM_LIMIT),
        name="outproj_mlp",
    )(x, attn, gmix, g1, sc2, sh2, g2, ba, bg, n2, fg, wo, w1, w2)


def _rope_tables(s):
    half = HEAD_DIM // 2
    inv_freq = ROPE_THETA ** (-jnp.arange(half, dtype=F32) / half)
    ang = jnp.arange(s).astype(F32)[:, None] * inv_freq[None, :]
    cos = jnp.cos(ang)
    sin = jnp.sin(ang)
    cos_t = jnp.tile(jnp.concatenate([cos, cos], axis=-1), (1, LANES // HEAD_DIM))
    sin_t = jnp.tile(jnp.concatenate([-sin, sin], axis=-1), (1, LANES // HEAD_DIM))
    return cos_t, sin_t


def kernel(x, c, ada_w, ada_b, norm1_g, w_in, w_spatial, b_spatial, gm_norm_g,
           beta_attn, beta_gmlp, w_out, norm2_g, w_ff1, w_ff2, final_g):
    b, s, d = x.shape
    depth = ada_w.shape[0]
    assert d == 2 * WIDTH and w_in.shape[2] == 6 * WIDTH + HEAD_DIM + N_HEADS
    cos_t, sin_t = _rope_tables(s)
    c_pad = jnp.zeros((8, d), F32).at[:b].set(c)
    kw0 = 4 * WIDTH
    kw1 = kw0 + HEAD_DIM + N_HEADS
    for l in range(depth):
        mod = _mod_call(c_pad, ada_w, ada_b[:, None, :], l)[:b]
        sh1, sc1, g1, sh2, sc2, g2 = [m[:, None, :] for m in jnp.split(mod, 6, axis=-1)]
        w_main = jnp.concatenate([w_in[l][:, :kw0], w_in[l][:, kw1:]], axis=1).astype(BF16)
        w_kw = jnp.pad(w_in[l][:, kw0:kw1], ((0, 0), (0, LANES - (kw1 - kw0)))).astype(BF16)
        bsp = jnp.repeat(b_spatial[l].reshape(N_PAIRS, 2, CHUNK).transpose(0, 2, 1),
                         HEAD_DIM, axis=2)
        qT, k, vT, qiT, ki, wT, gmix = _inproj_call(
            x, sc1, sh1, norm1_g[l][None, :], w_main, w_kw, cos_t, sin_t,
            w_spatial[l], bsp, gm_norm_g[l][None, :])
        attn = _attn_call(qT, qiT, wT, k, vT, ki)
        x = _out_call(x, attn, gmix, g1, sc2, sh2, g2,
                      beta_attn[l][None, :], beta_gmlp[l][None, :], norm2_g[l][None, :],
                      final_g[None, :], w_out[l].astype(BF16), w_ff1[l].astype(BF16),
                      w_ff2[l].astype(BF16), apply_final=(l == depth - 1))
    return x
```

```python
import functools

import jax
import jax.numpy as jnp
from jax import lax
from jax.experimental import pallas as pl
from jax.experimental.pallas import tpu as pltpu

F32 = jnp.float32
BF16 = jnp.bfloat16

LANES = 128
HEAD_DIM = 64
N_HEADS = 8
N_PAIRS = N_HEADS // 2
WIDTH = N_HEADS * HEAD_DIM
CHUNK = 128
TOPK = 256
ROPE_THETA = 10000.0
EPS = 1e-6
LOG2E = 1.4426950408889634
NEG_BIG = -1e30
LOWEST = -3.0e38
VMEM_LIMIT = 56 * 1024 * 1024

TM_IN = 512
TQ = 256
SUBLANES = 8
KB_IDX = 128
CK_IDX = 512
KB_TIE = 512
COARSE_BITS = 16
FINE_BLIND = 8
CK_CNT = 512
KB_CNT = 128
CK_ATT = 512
KB_ATT = 128
V_ROWS = 80
TM_OUT = 512
FF_CHUNK = 1024


def _gelu_tanh(x):
    return 0.5 * x * (1.0 + jnp.tanh(0.7978845608028654 * (x + 0.044715 * x * x * x)))


def _mod_kernel(c_ref, w_ref, b_ref, o_ref):
    c = c_ref[...]
    ca = c / (1.0 + jnp.exp(-c))
    w = w_ref[0]
    ca_hi = ca.astype(BF16)
    ca_lo = (ca - ca_hi.astype(F32)).astype(BF16)
    w_hi = w.astype(BF16)
    w_lo = (w - w_hi.astype(F32)).astype(BF16)
    dot = functools.partial(jnp.dot, preferred_element_type=F32)
    o_ref[...] = dot(ca_hi, w_hi) + (dot(ca_hi, w_lo) + dot(ca_lo, w_hi)) + b_ref[0]


def _mod_call(c_pad, w, b, layer):
    rows, d = c_pad.shape
    n = w.shape[2]
    tn = 1024
    return pl.pallas_call(
        _mod_kernel,
        grid=(n // tn,),
        in_specs=[pl.BlockSpec((rows, d), lambda j: (0, 0)),
                  pl.BlockSpec((1, d, tn), lambda j: (layer, 0, j)),
                  pl.BlockSpec((1, 1, tn), lambda j: (layer, 0, j))],
        out_specs=pl.BlockSpec((rows, tn), lambda j: (0, j)),
        out_shape=jax.ShapeDtypeStruct((rows, n), F32),
        name="adaln_mod",
    )(c_pad, w, b)


def _inproj_kernel(x_ref, sc_ref, sh_ref, g_ref, wm_ref, wk_ref, cos_ref, sin_ref,
                   wsp_ref, bsp_ref, gmg_ref,
                   qT_ref, k_ref, vT_ref, qiT_ref, ki_ref, wT_ref, gm_ref):
    tm = x_ref.shape[1]
    x = x_ref[0]
    ms = jnp.mean(x * x, axis=-1, keepdims=True)
    h = x * lax.rsqrt(ms + EPS) * g_ref[...]
    h = h * (1.0 + sc_ref[0]) + sh_ref[0]
    hb = h.astype(BF16)

    cos = cos_ref[...]
    sin = sin_ref[...]
    lane = lax.broadcasted_iota(jnp.int32, (tm, LANES), 1)
    first_half = (lane & (HEAD_DIM // 2)) == 0
    low_head = lane < HEAD_DIM

    def rope(t):
        partner = jnp.where(first_half, pltpu.roll(t, LANES - HEAD_DIM // 2, 1),
                            pltpu.roll(t, HEAD_DIM // 2, 1))
        return t * cos + partner * sin

    def proj(col, width=WIDTH):
        return jnp.dot(hb, wm_ref[:, col:col + width], preferred_element_type=F32)

    pq = proj(0)
    for j in range(N_PAIRS):
        sl = slice(j * LANES, (j + 1) * LANES)
        qT_ref[0, sl, :] = (rope(pq[:, sl]) * (HEAD_DIM ** -0.5 * LOG2E)).T.astype(BF16)
    pk = proj(WIDTH)
    for j in range(N_PAIRS):
        sl = slice(j * LANES, (j + 1) * LANES)
        k_ref[0, :, sl] = rope(pk[:, sl]).astype(BF16)
    pv = proj(2 * WIDTH)
    ones_row = jnp.where(lane == HEAD_DIM, 1.0, 0.0)
    for j in range(N_PAIRS):
        pair = pv[:, j * LANES:(j + 1) * LANES]
        vT_ref[0, 2 * j] = jnp.where(low_head, pair, ones_row).T[0:V_ROWS].astype(BF16)
        vT_ref[0, 2 * j + 1] = jnp.where(
            low_head, pltpu.roll(pair, HEAD_DIM, 1), ones_row).T[0:V_ROWS].astype(BF16)
    pqi = proj(3 * WIDTH)
    for j in range(N_PAIRS):
        sl = slice(j * LANES, (j + 1) * LANES)
        qiT_ref[0, sl, :] = (rope(pqi[:, sl]) * (HEAD_DIM ** -0.5)).T.astype(BF16)

    pkw = jnp.dot(hb, wk_ref[...], preferred_element_type=F32)
    rk = rope(pkw)
    ki_ref[0] = jnp.where(low_head, rk, pltpu.roll(rk, HEAD_DIM, 1)).astype(BF16)
    wT_ref[0] = pkw.T[HEAD_DIM:HEAD_DIM + N_HEADS, :] * (N_HEADS ** -0.5)

    gu = proj(4 * WIDTH)
    gv = proj(5 * WIDTH)
    lane_c = lax.broadcasted_iota(jnp.int32, (CHUNK, LANES), 1)
    row_c = lax.broadcasted_iota(jnp.int32, (CHUNK, LANES), 0)
    low_c = lane_c < HEAD_DIM
    causal = lane_c <= row_c
    inv_n = 1.0 / HEAD_DIM
    for j in range(N_PAIRS):
        sl = slice(j * LANES, (j + 1) * LANES)
        w_lo = jnp.where(causal, wsp_ref[2 * j], 0.0).astype(BF16)
        w_hi = jnp.where(causal, wsp_ref[2 * j + 1], 0.0).astype(BF16)
        gain = gmg_ref[:, sl]
        bias = bsp_ref[j]
        for cidx in range(tm // CHUNK):
            rs = slice(cidx * CHUNK, (cidx + 1) * CHUNK)
            vv = _gelu_tanh(gv[rs, sl])
            s_all = jnp.sum(vv, axis=-1, keepdims=True)
            s_lo = jnp.sum(jnp.where(low_c, vv, 0.0), axis=-1, keepdims=True)
            mu = jnp.where(low_c, s_lo, s_all - s_lo) * inv_n
            dv = vv - mu
            d2 = dv * dv
            q_all = jnp.sum(d2, axis=-1, keepdims=True)
            q_lo = jnp.sum(jnp.where(low_c, d2, 0.0), axis=-1, keepdims=True)
            var = jnp.where(low_c, q_lo, q_all - q_lo) * inv_n
            vn = (dv * lax.rsqrt(var + EPS) * gain).astype(BF16)
            m_lo = jnp.dot(w_lo, vn, preferred_element_type=F32)
            m_hi = jnp.dot(w_hi, vn, preferred_element_type=F32)
            mixed = jnp.where(low_c, m_lo, m_hi) + bias
            gm_ref[0, rs, sl] = _gelu_tanh(gu[rs, sl]) * mixed


def _inproj_call(x, sc1, sh1, g1n, w_main, w_kw, cos_t, sin_t, wsp, bsp, gmg):
    b, s, d = x.shape
    tm = TM_IN
    const2 = lambda bi, ti: (0, 0)
    const3 = lambda bi, ti: (0, 0, 0)
    tok = lambda bi, ti: (bi, ti, 0)
    tok_t = lambda bi, ti: (bi, 0, ti)
    return pl.pallas_call(
        _inproj_kernel,
        grid=(b, s // tm),
        in_specs=[pl.BlockSpec((1, tm, d), tok),
                  pl.BlockSpec((1, 1, d), lambda bi, ti: (bi, 0, 0)),
                  pl.BlockSpec((1, 1, d), lambda bi, ti: (bi, 0, 0)),
                  pl.BlockSpec((1, d), const2),
                  pl.BlockSpec(w_main.shape, const2),
                  pl.BlockSpec(w_kw.shape, const2),
                  pl.BlockSpec((tm, LANES), lambda bi, ti: (ti, 0)),
                  pl.BlockSpec((tm, LANES), lambda bi, ti: (ti, 0)),
                  pl.BlockSpec(wsp.shape, const3),
                  pl.BlockSpec(bsp.shape, const3),
                  pl.BlockSpec(gmg.shape, const2)],
        out_specs=[pl.BlockSpec((1, WIDTH, tm), tok_t),
                   pl.BlockSpec((1, tm, WIDTH), tok),
                   pl.BlockSpec((1, N_HEADS, V_ROWS, tm), lambda bi, ti: (bi, 0, 0, ti)),
                   pl.BlockSpec((1, WIDTH, tm), tok_t),
                   pl.BlockSpec((1, tm, LANES), tok),
                   pl.BlockSpec((1, N_HEADS, tm), tok_t),
                   pl.BlockSpec((1, tm, WIDTH), tok)],
        out_shape=[jax.ShapeDtypeStruct((b, WIDTH, s), BF16),
                   jax.ShapeDtypeStruct((b, s, WIDTH), BF16),
                   jax.ShapeDtypeStruct((b, N_HEADS, V_ROWS, s), BF16),
                   jax.ShapeDtypeStruct((b, WIDTH, s), BF16),
                   jax.ShapeDtypeStruct((b, s, LANES), BF16),
                   jax.ShapeDtypeStruct((b, N_HEADS, s), F32),
                   jax.ShapeDtypeStruct((b, s, WIDTH), F32)],
        compiler_params=pltpu.CompilerParams(
            dimension_semantics=("parallel", "parallel"), vmem_limit_bytes=VMEM_LIMIT),
        name="inproj_gmlp",
    )(x, sc1, sh1, g1n, w_main, w_kw, cos_t, sin_t, wsp, bsp, gmg)


def _tiles(x, rows=SUBLANES):
    return x.reshape(x.shape[0] // rows, rows, x.shape[1])


def _fold(x3, op, chains=4):
    accs = [x3[i] for i in range(min(chains, x3.shape[0]))]
    for i in range(len(accs), x3.shape[0]):
        accs[i % chains] = op(accs[i % chains], x3[i])
    while len(accs) > 1:
        accs = [op(accs[i], accs[i + 1]) if i + 1 < len(accs) else accs[i]
                for i in range(0, len(accs), 2)]
    return accs[0]


def _attn_kernel(qT_ref, qiT_ref, wT_ref, k_ref, vT_ref, ki_ref, o_ref,
                 isc_ref, isc16_ref, qm_ref, qim_ref,
                 thr_ref, lo_ref, width_ref, done_ref, tie_ref,
                 bias_ref, s_ref, p_ref, alpha_ref, m_ref, acc_ref):
    i = pl.program_id(1)
    q0 = i * TQ
    state = (SUBLANES, TQ)
    top_half = lax.broadcasted_iota(jnp.int32, (LANES, TQ), 0) < HEAD_DIM

    def replicate(row):
        return jnp.broadcast_to(row, state)

    qT = qT_ref[0].astype(F32)
    qiT = qiT_ref[0].astype(F32)
    for h in range(N_HEADS):
        sl = slice((h // 2) * LANES, (h // 2 + 1) * LANES)
        keep = top_half if h % 2 == 0 else jnp.logical_not(top_half)
        qm_ref[h] = jnp.where(keep, qT[sl, :], 0.0).astype(BF16)
        qim_ref[h] = jnp.where(keep, qiT[sl, :], 0.0).astype(BF16)

    n_idx_full = q0 // CK_IDX
    key_iota = lax.broadcasted_iota(jnp.int32, (KB_IDX, TQ), 0)
    qry_iota = lax.broadcasted_iota(jnp.int32, (KB_IDX, TQ), 1)

    def idx_keys(off, nkeys, masked):
        for r in range(nkeys // KB_IDX):
            koff = pl.multiple_of(off + r * KB_IDX, KB_IDX)
            kic = ki_ref[0, pl.ds(koff, KB_IDX), :]
            a = None
            for h in range(N_HEADS):
                lg = jnp.dot(kic, qim_ref[h], preferred_element_type=F32)
                t = jnp.maximum(lg, 0.0) * wT_ref[0, h:h + 1, :]
                a = t if a is None else a + t
            if masked:
                a = jnp.where(koff + key_iota <= q0 + qry_iota, a, -jnp.inf)
            isc_ref[pl.ds(koff, KB_IDX), :] = a
            isc16_ref[pl.ds(koff, KB_IDX), :] = a.astype(BF16)

    def idx_body(c, carry):
        idx_keys(pl.multiple_of(c * CK_IDX, CK_IDX), CK_IDX, False)
        return carry

    assert CK_IDX % TQ == 0 and TQ % KB_IDX == 0
    lax.fori_loop(0, n_idx_full, idx_body, 0)
    for t in range(CK_IDX // TQ - 1):
        @pl.when(q0 - n_idx_full * CK_IDX > t * TQ)
        def _leftover(t=t):
            idx_keys(pl.multiple_of(n_idx_full * CK_IDX + t * TQ, TQ), TQ, False)
    idx_keys(pl.multiple_of(q0, TQ), TQ, True)

    sweep = max(CK_CNT, CK_ATT)
    assert sweep % TQ == 0 and sweep % min(CK_CNT, CK_ATT) == 0

    for t in range(1, sweep // TQ):
        @pl.when((q0 + t * TQ) // sweep == q0 // sweep)
        def _pad_tail(t=t):
            off = pl.multiple_of(q0 + t * TQ, TQ)
            isc_ref[pl.ds(off, TQ), :] = jnp.full((TQ, TQ), -jnp.inf, F32)
            isc16_ref[pl.ds(off, TQ), :] = jnp.full((TQ, TQ), -jnp.inf, BF16)

    n_cnt = q0 // CK_CNT + 1
    needs_search = (q0 + lax.broadcasted_iota(jnp.int32, state, 1) + 1) > TOPK
    thr_ref[...] = jnp.full(state, LOWEST, F32)

    @pl.when(q0 + TQ > TOPK)
    def _search():
        int_min = jnp.int32(-2 ** 31)
        flip = jnp.int32(0x7FFFFFFF)
        high_bits = jnp.int32(-(1 << (32 - COARSE_BITS)))
        half_bucket = 1 << (31 - COARSE_BITS)

        def to_float(u):
            key = u ^ int_min
            return lax.bitcast_convert_type(jnp.where(key >= 0, key, key ^ flip), F32)

        def from_float(f):
            bits = lax.bitcast_convert_type(f, jnp.int32)
            return jnp.where(bits >= 0, bits, bits ^ flip) ^ int_min

        def bucket_value(u):
            bits = lax.bitcast_convert_type(to_float(u), jnp.int32) & high_bits
            return lax.bitcast_convert_type(bits, F32)

        def key_total(x):
            return replicate(jnp.sum(x, axis=0, keepdims=True))

        def count(src_ref, thr_tile, strict=False):
            rows = thr_tile.shape[0]
            one = jnp.ones((), thr_tile.dtype)
            zero = jnp.zeros((), thr_tile.dtype)

            def cnt_body(c, acc):
                off = pl.multiple_of(c * CK_CNT, CK_CNT)
                part = None
                for r in range(CK_CNT // KB_CNT):
                    koff = pl.multiple_of(off + r * KB_CNT, KB_CNT)
                    blk = _tiles(src_ref[pl.ds(koff, KB_CNT), :], rows)
                    hit = blk > thr_tile[None] if strict else blk >= thr_tile[None]
                    ones = _fold(jnp.where(hit, one, zero), jnp.add)
                    part = ones if part is None else part + ones
                return acc + part.astype(F32)

            acc = lax.fori_loop(0, n_cnt, cnt_body, jnp.zeros((rows, TQ), F32))
            return key_total(acc)

        lo_ref[...] = jnp.zeros(state, jnp.int32)

        def coarse_body(step, carry):
            cand = lo_ref[...] | lax.shift_right_logical(int_min, step)
            bucket = bucket_value(cand)
            thr_tile = jnp.concatenate([bucket, bucket], axis=0).astype(BF16)
            cnt = count(isc16_ref, thr_tile)
            lo_ref[...] = jnp.where(cnt >= float(TOPK), cand, lo_ref[...])
            return carry

        lax.fori_loop(0, COARSE_BITS, coarse_body, 0)

        bucket = bucket_value(lo_ref[...])
        exponent = lax.bitcast_convert_type(bucket, jnp.int32) & jnp.int32(0x7F800000)
        zero_tie = jnp.logical_and(needs_search, exponent == 0)
        tie_ref[...] = jnp.where(zero_tie, 1.0, 0.0)
        thr_ref[...] = jnp.where(zero_tie, 0.0, LOWEST)
        done_ref[...] = jnp.where(
            jnp.logical_and(needs_search, jnp.logical_not(zero_tie)), 0.0, 1.0)

        lo_ref[...] = from_float(bucket) - half_bucket
        width_ref[...] = jnp.full(state, 3 * half_bucket, jnp.int32)

        def cond(carry):
            step, active = carry
            return jnp.logical_and(step < COARSE_BITS + 4, active > 0.0)

        def fine_step():
            lo = lo_ref[...]
            width = width_ref[...]
            done = done_ref[...]
            half = lax.shift_right_logical(width, 1)
            mid = lo + half
            thr = to_float(mid)
            cnt = count(isc_ref, thr)
            active = done == 0.0
            ge = cnt >= float(TOPK)
            new_lo = jnp.where(ge, mid, lo)
            new_width = jnp.where(ge, width - half, half)
            hit = jnp.logical_and(active, cnt == float(TOPK))
            closed = jnp.logical_and(
                active, jnp.logical_and(jnp.logical_not(hit), new_width <= 1))
            thr_ref[...] = jnp.where(
                hit, thr, jnp.where(closed, to_float(new_lo), thr_ref[...]))
            tie_ref[...] = jnp.where(closed, 1.0, tie_ref[...])
            lo_ref[...] = jnp.where(active, new_lo, lo)
            width_ref[...] = jnp.where(active, new_width, width)
            done = jnp.where(jnp.logical_or(hit, closed), 1.0, done)
            done_ref[...] = done
            return done

        def blind_body(step, carry):
            fine_step()
            return carry

        def fine_body(carry):
            step, _ = carry
            return step + 1, jnp.max(1.0 - fine_step())

        lax.fori_loop(0, FINE_BLIND, blind_body, 0)
        lax.while_loop(cond, fine_body,
                       (jnp.int32(FINE_BLIND), jnp.max(1.0 - done_ref[...])))

        @pl.when(jnp.max(tie_ref[...]) > 0.0)
        def _ties():
            thr = thr_ref[...]
            quota = jnp.where(tie_ref[...] > 0.0,
                              float(TOPK) - count(isc_ref, thr, strict=True), 3.0e38)
            sq_r = lax.broadcasted_iota(jnp.int32, (KB_TIE, KB_TIE), 0)
            sq_c = lax.broadcasted_iota(jnp.int32, (KB_TIE, KB_TIE), 1)
            prefix = jnp.where(sq_c <= sq_r, 1.0, 0.0).astype(BF16)

            def tie_body(c, seen):
                off = pl.multiple_of(c * KB_TIE, KB_TIE)
                blk = _tiles(isc_ref[pl.ds(off, KB_TIE), :])
                tied = blk == thr[None]
                tied_f = jnp.where(tied, 1.0, 0.0)
                rank = seen[None] + _tiles(jnp.dot(
                    prefix, tied_f.reshape(KB_TIE, TQ).astype(BF16),
                    preferred_element_type=F32))
                drop = jnp.logical_and(tied, rank > quota[None])
                isc_ref[pl.ds(off, KB_TIE), :] = jnp.where(
                    drop, -jnp.inf, blk).reshape(KB_TIE, TQ)
                return seen + key_total(_fold(tied_f, jnp.add))

            lax.fori_loop(0, n_cnt * (CK_CNT // KB_TIE), tie_body, jnp.zeros(state, F32))

    m_ref[...] = jnp.full(m_ref.shape, NEG_BIG, F32)
    acc_ref[...] = jnp.zeros(acc_ref.shape, F32)
    n_att = q0 // CK_ATT + 1

    blocks = [slice(r * KB_ATT, (r + 1) * KB_ATT) for r in range(CK_ATT // KB_ATT)]

    def scores(c, buf, prev):
        off = pl.multiple_of(c * CK_ATT, CK_ATT)
        thr = thr_ref[...]
        for rs in blocks:
            rows = pl.ds(pl.multiple_of(off + rs.start, KB_ATT), KB_ATT)
            bias_ref[buf, rs, :] = jnp.where(_tiles(isc_ref[rows, :]) >= thr[None],
                                             0.0, NEG_BIG).reshape(KB_ATT, TQ)
        for h in range(N_HEADS):
            j = h // 2
            tile_max = None
            for rs in blocks:
                rows = pl.ds(pl.multiple_of(off + rs.start, KB_ATT), KB_ATT)
                kk = k_ref[0, rows, j * LANES:(j + 1) * LANES]
                s = jnp.dot(kk, qm_ref[h], preferred_element_type=F32) + bias_ref[buf, rs, :]
                s_ref[buf, h, rs, :] = s
                mx = _fold(_tiles(s), jnp.maximum)
                tile_max = mx if tile_max is None else jnp.maximum(tile_max, mx)
            m_old = m_ref[prev, h]
            m_new = jnp.maximum(m_old, replicate(jnp.max(tile_max, axis=0, keepdims=True)))
            alpha_ref[buf, h] = jnp.exp2(m_old - m_new)
            m_ref[buf, h] = m_new

    def values(c, buf):
        off = pl.multiple_of(c * CK_ATT, CK_ATT)
        for h in range(N_HEADS):
            m_new = m_ref[buf, h]
            for rs in blocks:
                d = (_tiles(s_ref[buf, h, rs, :]) - m_new[None]).reshape(KB_ATT, TQ)
                p_ref[h, rs, :] = jnp.exp2(d.astype(BF16))
        for h in range(N_HEADS):
            vv = vT_ref[0, h, :, pl.ds(off, CK_ATT)]
            pv = jnp.dot(vv, p_ref[h], preferred_element_type=F32)
            acc_ref[h] = alpha_ref[buf, h][0:1, :] * acc_ref[h] + pv

    def half_step(c, buf, prev):
        scores(c, buf, prev)
        values(c - 1, prev)

    def pair_body(jp, carry):
        half_step(2 * jp + 1, 1, 0)
        half_step(2 * jp + 2, 0, 1)
        return carry

    n_rest = n_att - 1
    scores(0, 0, 1)
    lax.fori_loop(0, n_rest // 2, pair_body, 0)

    @pl.when(n_rest % 2 == 1)
    def _odd_tail():
        half_step(n_rest, 1, 0)
        values(n_rest, 1)

    @pl.when(n_rest % 2 == 0)
    def _even_tail():
        values(n_rest, 0)

    def normalized(h):
        acc = acc_ref[h]
        return acc[0:HEAD_DIM, :] / acc[HEAD_DIM:HEAD_DIM + 1, :]

    for j in range(N_PAIRS):
        out_t = jnp.concatenate([normalized(2 * j), normalized(2 * j + 1)], axis=0)
        o_ref[0, :, j * LANES:(j + 1) * LANES] = out_t.T


def _attn_call(qT, qiT, wT, k, vT, ki):
    b, s, _ = k.shape
    tok = lambda bi, ti: (bi, ti, 0)
    tok_t = lambda bi, ti: (bi, 0, ti)
    per_batch = lambda bi, ti: (bi, 0, 0)
    resident = functools.partial(pl.BlockSpec, pipeline_mode=pl.Buffered(1))
    state = pltpu.VMEM((SUBLANES, TQ), F32)
    state_i = pltpu.VMEM((SUBLANES, TQ), jnp.int32)
    per_head = pltpu.VMEM((2, N_HEADS, SUBLANES, TQ), F32)
    return pl.pallas_call(
        _attn_kernel,
        grid=(b, s // TQ),
        in_specs=[pl.BlockSpec((1, WIDTH, TQ), tok_t),
                  pl.BlockSpec((1, WIDTH, TQ), tok_t),
                  pl.BlockSpec((1, N_HEADS, TQ), tok_t),
                  resident((1, s, WIDTH), per_batch),
                  resident((1, N_HEADS, V_ROWS, s), lambda bi, ti: (bi, 0, 0, 0)),
                  resident((1, s, LANES), per_batch)],
        out_specs=pl.BlockSpec((1, TQ, WIDTH), tok),
        out_shape=jax.ShapeDtypeStruct((b, s, WIDTH), F32),
        scratch_shapes=[pltpu.VMEM((s, TQ), F32),
                        pltpu.VMEM((s, TQ), BF16),
                        pltpu.VMEM((N_HEADS, LANES, TQ), BF16),
                        pltpu.VMEM((N_HEADS, LANES, TQ), BF16),
                        state,
                        state_i,
                        state_i,
                        state,
                        state,
                        pltpu.VMEM((2, CK_ATT, TQ), F32),
                        pltpu.VMEM((2, N_HEADS, CK_ATT, TQ), F32),
                        pltpu.VMEM((N_HEADS, CK_ATT, TQ), BF16),
                        per_head,
                        per_head,
                        pltpu.VMEM((N_HEADS, V_ROWS, TQ), F32)],
        compiler_params=pltpu.CompilerParams(
            dimension_semantics=("parallel", "arbitrary"), vmem_limit_bytes=VMEM_LIMIT),
        name="dsa_attention",
    )(qT, qiT, wT, k, vT, ki)


def _out_kernel(x_ref, a_ref, gmix_ref, g1_ref, sc2_ref, sh2_ref, g2_ref,
                ba_ref, bg_ref, n2_ref, fg_ref, wo_ref, w1_ref, w2_ref, o_ref,
                *, apply_final):
    def rms(t, g):
        return t * lax.rsqrt(jnp.mean(t * t, axis=-1, keepdims=True) + EPS) * g

    x = x_ref[0]
    a = rms(a_ref[0], ba_ref[...]).astype(BF16)
    g = rms(gmix_ref[0], bg_ref[...]).astype(BF16)
    y = (jnp.dot(a, wo_ref[0:WIDTH, :], preferred_element_type=F32)
         + jnp.dot(g, wo_ref[WIDTH:2 * WIDTH, :], preferred_element_type=F32))
    x1 = x + g1_ref[0] * y
    h2 = (rms(x1, n2_ref[...]) * (1.0 + sc2_ref[0]) + sh2_ref[0]).astype(BF16)
    ff = None
    d_ff = w1_ref.shape[1]
    for f in range(d_ff // FF_CHUNK):
        fs = slice(f * FF_CHUNK, (f + 1) * FF_CHUNK)
        u = jnp.maximum(jnp.dot(h2, w1_ref[:, fs], preferred_element_type=F32), 0.0)
        part = jnp.dot((u * u).astype(BF16), w2_ref[fs, :], preferred_element_type=F32)
        ff = part if ff is None else ff + part
    x2 = x1 + g2_ref[0] * ff
    o_ref[0] = rms(x2, fg_ref[...]) if apply_final else x2


def _out_call(x, attn, gmix, g1, sc2, sh2, g2, ba, bg, n2, fg, wo, w1, w2, apply_final):
    b, s, d = x.shape
    tm = TM_OUT
    tok = lambda bi, ti: (bi, ti, 0)
    per_b = lambda bi, ti: (bi, 0, 0)
    const2 = lambda bi, ti: (0, 0)
    resident = functools.partial(pl.BlockSpec, pipeline_mode=pl.Buffered(1))
    vec_b = pl.BlockSpec((1, 1, d), per_b)
    return pl.pallas_call(
        functools.partial(_out_kernel, apply_final=apply_final),
        grid=(b, s // tm),
        in_specs=[pl.BlockSpec((1, tm, d), tok),
                  pl.BlockSpec((1, tm, WIDTH), tok),
                  pl.BlockSpec((1, tm, WIDTH), tok),
                  vec_b, vec_b, vec_b, vec_b,
                  pl.BlockSpec((1, WIDTH), const2),
                  pl.BlockSpec((1, WIDTH), const2),
                  pl.BlockSpec((1, d), const2),
                  pl.BlockSpec((1, d), const2),
                  resident(wo.shape, const2),
                  resident(w1.shape, const2),
                  resident(w2.shape, const2)],
        out_specs=pl.BlockSpec((1, tm, d), tok),
        out_shape=jax.ShapeDtypeStruct((b, s, d), F32),
        compiler_params=pltpu.CompilerParams(
            dimension_semantics=("parallel", "parallel"), vmem_limit_bytes=VMEM_LIMIT),
        name="outproj_mlp",
    )(x, attn, gmix, g1, sc2, sh2, g2, ba, bg, n2, fg, wo, w1, w2)


def _rope_tables(s):
    half = HEAD_DIM // 2
    inv_freq = ROPE_THETA ** (-jnp.arange(half, dtype=F32) / half)
    ang = jnp.arange(s).astype(F32)[:, None] * inv_freq[None, :]
    cos = jnp.cos(ang)
    sin = jnp.sin(ang)
    cos_t = jnp.tile(jnp.concatenate([cos, cos], axis=-1), (1, LANES // HEAD_DIM))
    sin_t = jnp.tile(jnp.concatenate([-sin, sin], axis=-1), (1, LANES // HEAD_DIM))
    return cos_t, sin_t


def kernel(x, c, ada_w, ada_b, norm1_g, w_in, w_spatial, b_spatial, gm_norm_g,
           beta_attn, beta_gmlp, w_out, norm2_g, w_ff1, w_ff2, final_g):
    b, s, d = x.shape
    depth = ada_w.shape[0]
    assert d == 2 * WIDTH and w_in.shape[2] == 6 * WIDTH + HEAD_DIM + N_HEADS
    cos_t, sin_t = _rope_tables(s)
    c_pad = jnp.zeros((8, d), F32).at[:b].set(c)
    kw0 = 4 * WIDTH
    kw1 = kw0 + HEAD_DIM + N_HEADS
    for l in range(depth):
        mod = _mod_call(c_pad, ada_w, ada_b[:, None, :], l)[:b]
        sh1, sc1, g1, sh2, sc2, g2 = [m[:, None, :] for m in jnp.split(mod, 6, axis=-1)]
        w_main = jnp.concatenate([w_in[l][:, :kw0], w_in[l][:, kw1:]], axis=1).astype(BF16)
        w_kw = jnp.pad(w_in[l][:, kw0:kw1], ((0, 0), (0, LANES - (kw1 - kw0)))).astype(BF16)
        bsp = jnp.repeat(b_spatial[l].reshape(N_PAIRS, 2, CHUNK).transpose(0, 2, 1),
                         HEAD_DIM, axis=2)
        qT, k, vT, qiT, ki, wT, gmix = _inproj_call(
            x, sc1, sh1, norm1_g[l][None, :], w_main, w_kw, cos_t, sin_t,
            w_spatial[l], bsp, gm_norm_g[l][None, :])
        attn = _attn_call(qT, qiT, wT, k, vT, ki)
        x = _out_call(x, attn, gmix, g1, sc2, sh2, g2,
                      beta_attn[l][None, :], beta_gmlp[l][None, :], norm2_g[l][None, :],
                      final_g[None, :], w_out[l].astype(BF16), w_ff1[l].astype(BF16),
                      w_ff2[l].astype(BF16), apply_final=(l == depth - 1))
    return x
```

```python
import functools

import jax
import jax.numpy as jnp
from jax import lax
from jax.experimental import pallas as pl
from jax.experimental.pallas import tpu as pltpu

F32 = jnp.float32
BF16 = jnp.bfloat16

LANES = 128
HEAD_DIM = 64
N_HEADS = 8
N_PAIRS = N_HEADS // 2
WIDTH = N_HEADS * HEAD_DIM
CHUNK = 128
TOPK = 256
ROPE_THETA = 10000.0
EPS = 1e-6
LOG2E = 1.4426950408889634
NEG_BIG = -1e30
LOWEST = -3.0e38
VMEM_LIMIT = 56 * 1024 * 1024

TM_IN = 1024
TQ = 256
SUBLANES = 8
KB_IDX = 128
CK_IDX = 1024
KB_TIE = 512
COARSE_BITS = 16
FINE_BLIND = 8
CK_CNT = 512
KB_CNT = 128
CK_ATT = 512
KB_ATT = 128
V_ROWS = 80
TM_OUT = 512
FF_CHUNK = 1024


def _gelu_tanh(x):
    return 0.5 * x * (1.0 + jnp.tanh(0.7978845608028654 * (x + 0.044715 * x * x * x)))


def _mod_kernel(c_ref, w_ref, b_ref, o_ref):
    c = c_ref[...]
    ca = c / (1.0 + jnp.exp(-c))
    w = w_ref[0]
    ca_hi = ca.astype(BF16)
    ca_lo = (ca - ca_hi.astype(F32)).astype(BF16)
    w_hi = w.astype(BF16)
    w_lo = (w - w_hi.astype(F32)).astype(BF16)
    dot = functools.partial(jnp.dot, preferred_element_type=F32)
    o_ref[...] = dot(ca_hi, w_hi) + (dot(ca_hi, w_lo) + dot(ca_lo, w_hi)) + b_ref[0]


def _mod_call(c_pad, w, b, layer):
    rows, d = c_pad.shape
    n = w.shape[2]
    tn = 1024
    return pl.pallas_call(
        _mod_kernel,
        grid=(n // tn,),
        in_specs=[pl.BlockSpec((rows, d), lambda j: (0, 0)),
                  pl.BlockSpec((1, d, tn), lambda j: (layer, 0, j)),
                  pl.BlockSpec((1, 1, tn), lambda j: (layer, 0, j))],
        out_specs=pl.BlockSpec((rows, tn), lambda j: (0, j)),
        out_shape=jax.ShapeDtypeStruct((rows, n), F32),
        name="adaln_mod",
    )(c_pad, w, b)


def _inproj_kernel(x_ref, sc_ref, sh_ref, g_ref, wm_ref, wk_ref, cos_ref, sin_ref,
                   wsp_ref, bsp_ref, gmg_ref,
                   qT_ref, k_ref, vT_ref, qiT_ref, ki_ref, wT_ref, gm_ref):
    tm = x_ref.shape[1]
    x = x_ref[0]
    ms = jnp.mean(x * x, axis=-1, keepdims=True)
    h = x * lax.rsqrt(ms + EPS) * g_ref[...]
    h = h * (1.0 + sc_ref[0]) + sh_ref[0]
    hb = h.astype(BF16)

    cos = cos_ref[...]
    sin = sin_ref[...]
    lane = lax.broadcasted_iota(jnp.int32, (tm, LANES), 1)
    first_half = (lane & (HEAD_DIM // 2)) == 0
    low_head = lane < HEAD_DIM

    def rope(t):
        partner = jnp.where(first_half, pltpu.roll(t, LANES - HEAD_DIM // 2, 1),
                            pltpu.roll(t, HEAD_DIM // 2, 1))
        return t * cos + partner * sin

    def proj(col, width=WIDTH):
        return jnp.dot(hb, wm_ref[:, col:col + width], preferred_element_type=F32)

    pq = proj(0)
    for j in range(N_PAIRS):
        sl = slice(j * LANES, (j + 1) * LANES)
        qT_ref[0, sl, :] = (rope(pq[:, sl]) * (HEAD_DIM ** -0.5 * LOG2E)).T.astype(BF16)
    pk = proj(WIDTH)
    for j in range(N_PAIRS):
        sl = slice(j * LANES, (j + 1) * LANES)
        k_ref[0, :, sl] = rope(pk[:, sl]).astype(BF16)
    pv = proj(2 * WIDTH)
    ones_row = jnp.where(lane == HEAD_DIM, 1.0, 0.0)
    for j in range(N_PAIRS):
        pair = pv[:, j * LANES:(j + 1) * LANES]
        vT_ref[0, 2 * j] = jnp.where(low_head, pair, ones_row).T[0:V_ROWS].astype(BF16)
        vT_ref[0, 2 * j + 1] = jnp.where(
            low_head, pltpu.roll(pair, HEAD_DIM, 1), ones_row).T[0:V_ROWS].astype(BF16)
    pqi = proj(3 * WIDTH)
    for j in range(N_PAIRS):
        sl = slice(j * LANES, (j + 1) * LANES)
        qiT_ref[0, sl, :] = (rope(pqi[:, sl]) * (HEAD_DIM ** -0.5)).T.astype(BF16)

    pkw = jnp.dot(hb, wk_ref[...], preferred_element_type=F32)
    rk = rope(pkw)
    ki_ref[0] = jnp.where(low_head, rk, pltpu.roll(rk, HEAD_DIM, 1)).astype(BF16)
    wT_ref[0] = pkw.T[HEAD_DIM:HEAD_DIM + N_HEADS, :] * (N_HEADS ** -0.5)

    gu = proj(4 * WIDTH)
    gv = proj(5 * WIDTH)
    lane_c = lax.broadcasted_iota(jnp.int32, (CHUNK, LANES), 1)
    row_c = lax.broadcasted_iota(jnp.int32, (CHUNK, LANES), 0)
    low_c = lane_c < HEAD_DIM
    causal = lane_c <= row_c
    inv_n = 1.0 / HEAD_DIM
    for j in range(N_PAIRS):
        sl = slice(j * LANES, (j + 1) * LANES)
        w_lo = jnp.where(causal, wsp_ref[2 * j], 0.0).astype(BF16)
        w_hi = jnp.where(causal, wsp_ref[2 * j + 1], 0.0).astype(BF16)
        gain = gmg_ref[:, sl]
        bias = bsp_ref[j]
        for cidx in range(tm // CHUNK):
            rs = slice(cidx * CHUNK, (cidx + 1) * CHUNK)
            vv = _gelu_tanh(gv[rs, sl])
            s_all = jnp.sum(vv, axis=-1, keepdims=True)
            s_lo = jnp.sum(jnp.where(low_c, vv, 0.0), axis=-1, keepdims=True)
            mu = jnp.where(low_c, s_lo, s_all - s_lo) * inv_n
            dv = vv - mu
            d2 = dv * dv
            q_all = jnp.sum(d2, axis=-1, keepdims=True)
            q_lo = jnp.sum(jnp.where(low_c, d2, 0.0), axis=-1, keepdims=True)
            var = jnp.where(low_c, q_lo, q_all - q_lo) * inv_n
            vn = (dv * lax.rsqrt(var + EPS) * gain).astype(BF16)
            m_lo = jnp.dot(w_lo, vn, preferred_element_type=F32)
            m_hi = jnp.dot(w_hi, vn, preferred_element_type=F32)
            mixed = jnp.where(low_c, m_lo, m_hi) + bias
            gm_ref[0, rs, sl] = _gelu_tanh(gu[rs, sl]) * mixed


def _inproj_call(x, sc1, sh1, g1n, w_main, w_kw, cos_t, sin_t, wsp, bsp, gmg):
    b, s, d = x.shape
    tm = TM_IN
    const2 = lambda bi, ti: (0, 0)
    const3 = lambda bi, ti: (0, 0, 0)
    tok = lambda bi, ti: (bi, ti, 0)
    tok_t = lambda bi, ti: (bi, 0, ti)
    return pl.pallas_call(
        _inproj_kernel,
        grid=(b, s // tm),
        in_specs=[pl.BlockSpec((1, tm, d), tok),
                  pl.BlockSpec((1, 1, d), lambda bi, ti: (bi, 0, 0)),
                  pl.BlockSpec((1, 1, d), lambda bi, ti: (bi, 0, 0)),
                  pl.BlockSpec((1, d), const2),
                  pl.BlockSpec(w_main.shape, const2),
                  pl.BlockSpec(w_kw.shape, const2),
                  pl.BlockSpec((tm, LANES), lambda bi, ti: (ti, 0)),
                  pl.BlockSpec((tm, LANES), lambda bi, ti: (ti, 0)),
                  pl.BlockSpec(wsp.shape, const3),
                  pl.BlockSpec(bsp.shape, const3),
                  pl.BlockSpec(gmg.shape, const2)],
        out_specs=[pl.BlockSpec((1, WIDTH, tm), tok_t),
                   pl.BlockSpec((1, tm, WIDTH), tok),
                   pl.BlockSpec((1, N_HEADS, V_ROWS, tm), lambda bi, ti: (bi, 0, 0, ti)),
                   pl.BlockSpec((1, WIDTH, tm), tok_t),
                   pl.BlockSpec((1, tm, LANES), tok),
                   pl.BlockSpec((1, N_HEADS, tm), tok_t),
                   pl.BlockSpec((1, tm, WIDTH), tok)],
        out_shape=[jax.ShapeDtypeStruct((b, WIDTH, s), BF16),
                   jax.ShapeDtypeStruct((b, s, WIDTH), BF16),
                   jax.ShapeDtypeStruct((b, N_HEADS, V_ROWS, s), BF16),
                   jax.ShapeDtypeStruct((b, WIDTH, s), BF16),
                   jax.ShapeDtypeStruct((b, s, LANES), BF16),
                   jax.ShapeDtypeStruct((b, N_HEADS, s), F32),
                   jax.ShapeDtypeStruct((b, s, WIDTH), F32)],
        compiler_params=pltpu.CompilerParams(
            dimension_semantics=("parallel", "parallel"), vmem_limit_bytes=VMEM_LIMIT),
        name="inproj_gmlp",
    )(x, sc1, sh1, g1n, w_main, w_kw, cos_t, sin_t, wsp, bsp, gmg)


def _tiles(x, rows=SUBLANES):
    return x.reshape(x.shape[0] // rows, rows, x.shape[1])


def _fold(x3, op, chains=4):
    accs = [x3[i] for i in range(min(chains, x3.shape[0]))]
    for i in range(len(accs), x3.shape[0]):
        accs[i % chains] = op(accs[i % chains], x3[i])
    while len(accs) > 1:
        accs = [op(accs[i], accs[i + 1]) if i + 1 < len(accs) else accs[i]
                for i in range(0, len(accs), 2)]
    return accs[0]


def _attn_kernel(qT_ref, qiT_ref, wT_ref, k_ref, vT_ref, ki_ref, o_ref,
                 isc_ref, isc16_ref, qm_ref, qim_ref,
                 thr_ref, lo_ref, width_ref, done_ref, tie_ref,
                 bias_ref, s_ref, p_ref, alpha_ref, m_ref, acc_ref):
    i = pl.program_id(1)
    q0 = i * TQ
    state = (SUBLANES, TQ)
    top_half = lax.broadcasted_iota(jnp.int32, (LANES, TQ), 0) < HEAD_DIM

    def replicate(row):
        return jnp.broadcast_to(row, state)

    qT = qT_ref[0].astype(F32)
    qiT = qiT_ref[0].astype(F32)
    for h in range(N_HEADS):
        sl = slice((h // 2) * LANES, (h // 2 + 1) * LANES)
        keep = top_half if h % 2 == 0 else jnp.logical_not(top_half)
        qm_ref[h] = jnp.where(keep, qT[sl, :], 0.0).astype(BF16)
        qim_ref[h] = jnp.where(keep, qiT[sl, :], 0.0).astype(BF16)

    n_idx_full = q0 // CK_IDX
    key_iota = lax.broadcasted_iota(jnp.int32, (KB_IDX, TQ), 0)
    qry_iota = lax.broadcasted_iota(jnp.int32, (KB_IDX, TQ), 1)

    def idx_keys(off, nkeys, masked):
        for r in range(nkeys // KB_IDX):
            koff = pl.multiple_of(off + r * KB_IDX, KB_IDX)
            kic = ki_ref[0, pl.ds(koff, KB_IDX), :]
            a = None
            for h in range(N_HEADS):
                lg = jnp.dot(kic, qim_ref[h], preferred_element_type=F32)
                t = jnp.maximum(lg, 0.0) * wT_ref[0, h:h + 1, :]
                a = t if a is None else a + t
            if masked:
                a = jnp.where(koff + key_iota <= q0 + qry_iota, a, -jnp.inf)
            isc_ref[pl.ds(koff, KB_IDX), :] = a
            isc16_ref[pl.ds(koff, KB_IDX), :] = a.astype(BF16)

    def idx_body(c, carry):
        idx_keys(pl.multiple_of(c * CK_IDX, CK_IDX), CK_IDX, False)
        return carry

    assert CK_IDX % TQ == 0 and TQ % KB_IDX == 0
    lax.fori_loop(0, n_idx_full, idx_body, 0)
    for t in range(CK_IDX // TQ - 1):
        @pl.when(q0 - n_idx_full * CK_IDX > t * TQ)
        def _leftover(t=t):
            idx_keys(pl.multiple_of(n_idx_full * CK_IDX + t * TQ, TQ), TQ, False)
    idx_keys(pl.multiple_of(q0, TQ), TQ, True)

    sweep = max(CK_CNT, CK_ATT)
    assert sweep % TQ == 0 and sweep % min(CK_CNT, CK_ATT) == 0

    for t in range(1, sweep // TQ):
        @pl.when((q0 + t * TQ) // sweep == q0 // sweep)
        def _pad_tail(t=t):
            off = pl.multiple_of(q0 + t * TQ, TQ)
            isc_ref[pl.ds(off, TQ), :] = jnp.full((TQ, TQ), -jnp.inf, F32)
            isc16_ref[pl.ds(off, TQ), :] = jnp.full((TQ, TQ), -jnp.inf, BF16)

    n_cnt = q0 // CK_CNT + 1
    needs_search = (q0 + lax.broadcasted_iota(jnp.int32, state, 1) + 1) > TOPK
    thr_ref[...] = jnp.full(state, LOWEST, F32)

    @pl.when(q0 + TQ > TOPK)
    def _search():
        int_min = jnp.int32(-2 ** 31)
        flip = jnp.int32(0x7FFFFFFF)
        high_bits = jnp.int32(-(1 << (32 - COARSE_BITS)))
        half_bucket = 1 << (31 - COARSE_BITS)

        def to_float(u):
            key = u ^ int_min
            return lax.bitcast_convert_type(jnp.where(key >= 0, key, key ^ flip), F32)

        def from_float(f):
            bits = lax.bitcast_convert_type(f, jnp.int32)
            return jnp.where(bits >= 0, bits, bits ^ flip) ^ int_min

        def bucket_value(u):
            bits = lax.bitcast_convert_type(to_float(u), jnp.int32) & high_bits
            return lax.bitcast_convert_type(bits, F32)

        def key_total(x):
            return replicate(jnp.sum(x, axis=0, keepdims=True))

        def count(src_ref, thr_tile, strict=False):
            rows = thr_tile.shape[0]
            one = jnp.ones((), thr_tile.dtype)
            zero = jnp.zeros((), thr_tile.dtype)

            def cnt_body(c, acc):
                off = pl.multiple_of(c * CK_CNT, CK_CNT)
                part = None
                for r in range(CK_CNT // KB_CNT):
                    koff = pl.multiple_of(off + r * KB_CNT, KB_CNT)
                    blk = _tiles(src_ref[pl.ds(koff, KB_CNT), :], rows)
                    hit = blk > thr_tile[None] if strict else blk >= thr_tile[None]
                    ones = _fold(jnp.where(hit, one, zero), jnp.add)
                    part = ones if part is None else part + ones
                return acc + part.astype(F32)

            acc = lax.fori_loop(0, n_cnt, cnt_body, jnp.zeros((rows, TQ), F32))
            return key_total(acc)

        lo_ref[...] = jnp.zeros(state, jnp.int32)

        def coarse_body(step, carry):
            cand = lo_ref[...] | lax.shift_right_logical(int_min, step)
            bucket = bucket_value(cand)
            thr_tile = jnp.concatenate([bucket, bucket], axis=0).astype(BF16)
            cnt = count(isc16_ref, thr_tile)
            lo_ref[...] = jnp.where(cnt >= float(TOPK), cand, lo_ref[...])
            return carry

        lax.fori_loop(0, COARSE_BITS, coarse_body, 0)

        bucket = bucket_value(lo_ref[...])
        exponent = lax.bitcast_convert_type(bucket, jnp.int32) & jnp.int32(0x7F800000)
        zero_tie = jnp.logical_and(needs_search, exponent == 0)
        tie_ref[...] = jnp.where(zero_tie, 1.0, 0.0)
        thr_ref[...] = jnp.where(zero_tie, 0.0, LOWEST)
        done_ref[...] = jnp.where(
            jnp.logical_and(needs_search, jnp.logical_not(zero_tie)), 0.0, 1.0)

        lo_ref[...] = from_float(bucket) - half_bucket
        width_ref[...] = jnp.full(state, 3 * half_bucket, jnp.int32)

        def cond(carry):
            step, active = carry
            return jnp.logical_and(step < COARSE_BITS + 4, active > 0.0)

        def fine_step():
            lo = lo_ref[...]
            width = width_ref[...]
            done = done_ref[...]
            half = lax.shift_right_logical(width, 1)
            mid = lo + half
            thr = to_float(mid)
            cnt = count(isc_ref, thr)
            active = done == 0.0
            ge = cnt >= float(TOPK)
            new_lo = jnp.where(ge, mid, lo)
            new_width = jnp.where(ge, width - half, half)
            hit = jnp.logical_and(active, cnt == float(TOPK))
            closed = jnp.logical_and(
                active, jnp.logical_and(jnp.logical_not(hit), new_width <= 1))
            thr_ref[...] = jnp.where(
                hit, thr, jnp.where(closed, to_float(new_lo), thr_ref[...]))
            tie_ref[...] = jnp.where(closed, 1.0, tie_ref[...])
            lo_ref[...] = jnp.where(active, new_lo, lo)
            width_ref[...] = jnp.where(active, new_width, width)
            done = jnp.where(jnp.logical_or(hit, closed), 1.0, done)
            done_ref[...] = done
            return done

        def blind_body(step, carry):
            fine_step()
            return carry

        def fine_body(carry):
            step, _ = carry
            return step + 1, jnp.max(1.0 - fine_step())

        lax.fori_loop(0, FINE_BLIND, blind_body, 0)
        lax.while_loop(cond, fine_body,
                       (jnp.int32(FINE_BLIND), jnp.max(1.0 - done_ref[...])))

        @pl.when(jnp.max(tie_ref[...]) > 0.0)
        def _ties():
            thr = thr_ref[...]
            quota = jnp.where(tie_ref[...] > 0.0,
                              float(TOPK) - count(isc_ref, thr, strict=True), 3.0e38)
            sq_r = lax.broadcasted_iota(jnp.int32, (KB_TIE, KB_TIE), 0)
            sq_c = lax.broadcasted_iota(jnp.int32, (KB_TIE, KB_TIE), 1)
            prefix = jnp.where(sq_c <= sq_r, 1.0, 0.0).astype(BF16)

            def tie_body(c, seen):
                off = pl.multiple_of(c * KB_TIE, KB_TIE)
                blk = _tiles(isc_ref[pl.ds(off, KB_TIE), :])
                tied = blk == thr[None]
                tied_f = jnp.where(tied, 1.0, 0.0)
                rank = seen[None] + _tiles(jnp.dot(
                    prefix, tied_f.reshape(KB_TIE, TQ).astype(BF16),
                    preferred_element_type=F32))
                drop = jnp.logical_and(tied, rank > quota[None])
                isc_ref[pl.ds(off, KB_TIE), :] = jnp.where(
                    drop, -jnp.inf, blk).reshape(KB_TIE, TQ)
                return seen + key_total(_fold(tied_f, jnp.add))

            lax.fori_loop(0, n_cnt * (CK_CNT // KB_TIE), tie_body, jnp.zeros(state, F32))

    m_ref[...] = jnp.full(m_ref.shape, NEG_BIG, F32)
    acc_ref[...] = jnp.zeros(acc_ref.shape, F32)
    n_att = q0 // CK_ATT + 1

    blocks = [slice(r * KB_ATT, (r + 1) * KB_ATT) for r in range(CK_ATT // KB_ATT)]

    def scores(c, buf, prev):
        off = pl.multiple_of(c * CK_ATT, CK_ATT)
        thr = thr_ref[...]
        for rs in blocks:
            rows = pl.ds(pl.multiple_of(off + rs.start, KB_ATT), KB_ATT)
            bias_ref[buf, rs, :] = jnp.where(_tiles(isc_ref[rows, :]) >= thr[None],
                                             0.0, NEG_BIG).reshape(KB_ATT, TQ)
        for h in range(N_HEADS):
            j = h // 2
            tile_max = None
            for rs in blocks:
                rows = pl.ds(pl.multiple_of(off + rs.start, KB_ATT), KB_ATT)
                kk = k_ref[0, rows, j * LANES:(j + 1) * LANES]
                s = jnp.dot(kk, qm_ref[h], preferred_element_type=F32) + bias_ref[buf, rs, :]
                s_ref[buf, h, rs, :] = s
                mx = _fold(_tiles(s), jnp.maximum)
                tile_max = mx if tile_max is None else jnp.maximum(tile_max, mx)
            m_old = m_ref[prev, h]
            m_new = jnp.maximum(m_old, replicate(jnp.max(tile_max, axis=0, keepdims=True)))
            alpha_ref[buf, h] = jnp.exp2(m_old - m_new)
            m_ref[buf, h] = m_new

    def values(c, buf):
        off = pl.multiple_of(c * CK_ATT, CK_ATT)
        for h in range(N_HEADS):
            m_new = m_ref[buf, h]
            for rs in blocks:
                d = (_tiles(s_ref[buf, h, rs, :]) - m_new[None]).reshape(KB_ATT, TQ)
                p_ref[h, rs, :] = jnp.exp2(d.astype(BF16))
        for h in range(N_HEADS):
            vv = vT_ref[0, h, :, pl.ds(off, CK_ATT)]
            pv = jnp.dot(vv, p_ref[h], preferred_element_type=F32)
            acc_ref[h] = alpha_ref[buf, h][0:1, :] * acc_ref[h] + pv

    def half_step(c, buf, prev):
        scores(c, buf, prev)
        values(c - 1, prev)

    def pair_body(jp, carry):
        half_step(2 * jp + 1, 1, 0)
        half_step(2 * jp + 2, 0, 1)
        return carry

    n_rest = n_att - 1
    scores(0, 0, 1)
    lax.fori_loop(0, n_rest // 2, pair_body, 0)

    @pl.when(n_rest % 2 == 1)
    def _odd_tail():
        half_step(n_rest, 1, 0)
        values(n_rest, 1)

    @pl.when(n_rest % 2 == 0)
    def _even_tail():
        values(n_rest, 0)

    def normalized(h):
        acc = acc_ref[h]
        return acc[0:HEAD_DIM, :] / acc[HEAD_DIM:HEAD_DIM + 1, :]

    for j in range(N_PAIRS):
        out_t = jnp.concatenate([normalized(2 * j), normalized(2 * j + 1)], axis=0)
        o_ref[0, :, j * LANES:(j + 1) * LANES] = out_t.T


def _attn_call(qT, qiT, wT, k, vT, ki):
    b, s, _ = k.shape
    tok = lambda bi, ti: (bi, ti, 0)
    tok_t = lambda bi, ti: (bi, 0, ti)
    per_batch = lambda bi, ti: (bi, 0, 0)
    resident = functools.partial(pl.BlockSpec, pipeline_mode=pl.Buffered(1))
    state = pltpu.VMEM((SUBLANES, TQ), F32)
    state_i = pltpu.VMEM((SUBLANES, TQ), jnp.int32)
    per_head = pltpu.VMEM((2, N_HEADS, SUBLANES, TQ), F32)
    return pl.pallas_call(
        _attn_kernel,
        grid=(b, s // TQ),
        in_specs=[pl.BlockSpec((1, WIDTH, TQ), tok_t),
                  pl.BlockSpec((1, WIDTH, TQ), tok_t),
                  pl.BlockSpec((1, N_HEADS, TQ), tok_t),
                  resident((1, s, WIDTH), per_batch),
                  resident((1, N_HEADS, V_ROWS, s), lambda bi, ti: (bi, 0, 0, 0)),
                  resident((1, s, LANES), per_batch)],
        out_specs=pl.BlockSpec((1, TQ, WIDTH), tok),
        out_shape=jax.ShapeDtypeStruct((b, s, WIDTH), F32),
        scratch_shapes=[pltpu.VMEM((s, TQ), F32),
                        pltpu.VMEM((s, TQ), BF16),
                        pltpu.VMEM((N_HEADS, LANES, TQ), BF16),
                        pltpu.VMEM((N_HEADS, LANES, TQ), BF16),
                        state,
                        state_i,
                        state_i,
                        state,
                        state,
                        pltpu.VMEM((2, CK_ATT, TQ), F32),
                        pltpu.VMEM((2, N_HEADS, CK_ATT, TQ), F32),
                        pltpu.VMEM((N_HEADS, CK_ATT, TQ), BF16),
                        per_head,
                        per_head,
                        pltpu.VMEM((N_HEADS, V_ROWS, TQ), F32)],
        compiler_params=pltpu.CompilerParams(
            dimension_semantics=("parallel", "arbitrary"), vmem_limit_bytes=VMEM_LIMIT),
        name="dsa_attention",
    )(qT, qiT, wT, k, vT, ki)


def _out_kernel(x_ref, a_ref, gmix_ref, g1_ref, sc2_ref, sh2_ref, g2_ref,
                ba_ref, bg_ref, n2_ref, fg_ref, wo_ref, w1_ref, w2_ref, o_ref,
                *, apply_final):
    def rms(t, g):
        return t * lax.rsqrt(jnp.mean(t * t, axis=-1, keepdims=True) + EPS) * g

    x = x_ref[0]
    a = rms(a_ref[0], ba_ref[...]).astype(BF16)
    g = rms(gmix_ref[0], bg_ref[...]).astype(BF16)
    y = (jnp.dot(a, wo_ref[0:WIDTH, :], preferred_element_type=F32)
         + jnp.dot(g, wo_ref[WIDTH:2 * WIDTH, :], preferred_element_type=F32))
    x1 = x + g1_ref[0] * y
    h2 = (rms(x1, n2_ref[...]) * (1.0 + sc2_ref[0]) + sh2_ref[0]).astype(BF16)
    ff = None
    d_ff = w1_ref.shape[1]
    for f in range(d_ff // FF_CHUNK):
        fs = slice(f * FF_CHUNK, (f + 1) * FF_CHUNK)
        u = jnp.maximum(jnp.dot(h2, w1_ref[:, fs], preferred_element_type=F32), 0.0)
        part = jnp.dot((u * u).astype(BF16), w2_ref[fs, :], preferred_element_type=F32)
        ff = part if ff is None else ff + part
    x2 = x1 + g2_ref[0] * ff
    o_ref[0] = rms(x2, fg_ref[...]) if apply_final else x2


def _out_call(x, attn, gmix, g1, sc2, sh2, g2, ba, bg, n2, fg, wo, w1, w2, apply_final):
    b, s, d = x.shape
    tm = TM_OUT
    tok = lambda bi, ti: (bi, ti, 0)
    per_b = lambda bi, ti: (bi, 0, 0)
    const2 = lambda bi, ti: (0, 0)
    resident = functools.partial(pl.BlockSpec, pipeline_mode=pl.Buffered(1))
    vec_b = pl.BlockSpec((1, 1, d), per_b)
    return pl.pallas_call(
        functools.partial(_out_kernel, apply_final=apply_final),
        grid=(b, s // tm),
        in_specs=[pl.BlockSpec((1, tm, d), tok),
                  pl.BlockSpec((1, tm, WIDTH), tok),
                  pl.BlockSpec((1, tm, WIDTH), tok),
                  vec_b, vec_b, vec_b, vec_b,
                  pl.BlockSpec((1, WIDTH), const2),
                  pl.BlockSpec((1, WIDTH), const2),
                  pl.BlockSpec((1, d), const2),
                  pl.BlockSpec((1, d), const2),
                  resident(wo.shape, const2),
                  resident(w1.shape, const2),
                  resident(w2.shape, const2)],
        out_specs=pl.BlockSpec((1, tm, d), tok),
        out_shape=jax.ShapeDtypeStruct((b, s, d), F32),
        compiler_params=pltpu.CompilerParams(
            dimension_semantics=("parallel", "parallel"), vmem_limit_bytes=VMEM_LIMIT),
        name="outproj_mlp",
    )(x, attn, gmix, g1, sc2, sh2, g2, ba, bg, n2, fg, wo, w1, w2)


def _rope_tables(s):
    half = HEAD_DIM // 2
    inv_freq = ROPE_THETA ** (-jnp.arange(half, dtype=F32) / half)
    ang = jnp.arange(s).astype(F32)[:, None] * inv_freq[None, :]
    cos = jnp.cos(ang)
    sin = jnp.sin(ang)
    cos_t = jnp.tile(jnp.concatenate([cos, cos], axis=-1), (1, LANES // HEAD_DIM))
    sin_t = jnp.tile(jnp.concatenate([-sin, sin], axis=-1), (1, LANES // HEAD_DIM))
    return cos_t, sin_t


def kernel(x, c, ada_w, ada_b, norm1_g, w_in, w_spatial, b_spatial, gm_norm_g,
           beta_attn, beta_gmlp, w_out, norm2_g, w_ff1, w_ff2, final_g):
    b, s, d = x.shape
    depth = ada_w.shape[0]
    assert d == 2 * WIDTH and w_in.shape[2] == 6 * WIDTH + HEAD_DIM + N_HEADS
    cos_t, sin_t = _rope_tables(s)
    c_pad = jnp.zeros((8, d), F32).at[:b].set(c)
    kw0 = 4 * WIDTH
    kw1 = kw0 + HEAD_DIM + N_HEADS
    for l in range(depth):
        mod = _mod_call(c_pad, ada_w, ada_b[:, None, :], l)[:b]
        sh1, sc1, g1, sh2, sc2, g2 = [m[:, None, :] for m in jnp.split(mod, 6, axis=-1)]
        w_main = jnp.concatenate([w_in[l][:, :kw0], w_in[l][:, kw1:]], axis=1).astype(BF16)
        w_kw = jnp.pad(w_in[l][:, kw0:kw1], ((0, 0), (0, LANES - (kw1 - kw0)))).astype(BF16)
        bsp = jnp.repeat(b_spatial[l].reshape(N_PAIRS, 2, CHUNK).transpose(0, 2, 1),
                         HEAD_DIM, axis=2)
        qT, k, vT, qiT, ki, wT, gmix = _inproj_call(
            x, sc1, sh1, norm1_g[l][None, :], w_main, w_kw, cos_t, sin_t,
            w_spatial[l], bsp, gm_norm_g[l][None, :])
        attn = _attn_call(qT, qiT, wT, k, vT, ki)
        x = _out_call(x, attn, gmix, g1, sc2, sh2, g2,
                      beta_attn[l][None, :], beta_gmlp[l][None, :], norm2_g[l][None, :],
                      final_g[None, :], w_out[l].astype(BF16), w_ff1[l].astype(BF16),
                      w_ff2[l].astype(BF16), apply_final=(l == depth - 1))
    return x
```

```python
import functools

import jax
import jax.numpy as jnp
from jax import lax
from jax.experimental import pallas as pl
from jax.experimental.pallas import tpu as pltpu

F32 = jnp.float32
BF16 = jnp.bfloat16

LANES = 128
HEAD_DIM = 64
N_HEADS = 8
N_PAIRS = N_HEADS // 2
WIDTH = N_HEADS * HEAD_DIM
CHUNK = 128
TOPK = 256
ROPE_THETA = 10000.0
EPS = 1e-6
LOG2E = 1.4426950408889634
NEG_BIG = -1e30
LOWEST = -3.0e38
VMEM_LIMIT = 56 * 1024 * 1024

TM_IN = 1024
TQ = 256
SUBLANES = 8
KB_IDX = 128
CK_IDX = 1024
KB_TIE = 512
COARSE_BITS = 16
FINE_BLIND = 8
CK_CNT = 512
KB_CNT = 128
CK_ATT = 512
KB_ATT = 128
V_ROWS = 80
TM_OUT = 512
FF_CHUNK = 1024


def _gelu_tanh(x):
    return 0.5 * x * (1.0 + jnp.tanh(0.7978845608028654 * (x + 0.044715 * x * x * x)))


def _mod_kernel(c_ref, w_ref, b_ref, o_ref):
    c = c_ref[...]
    ca = c / (1.0 + jnp.exp(-c))
    w = w_ref[0]
    ca_hi = ca.astype(BF16)
    ca_lo = (ca - ca_hi.astype(F32)).astype(BF16)
    w_hi = w.astype(BF16)
    w_lo = (w - w_hi.astype(F32)).astype(BF16)
    dot = functools.partial(jnp.dot, preferred_element_type=F32)
    o_ref[...] = dot(ca_hi, w_hi) + (dot(ca_hi, w_lo) + dot(ca_lo, w_hi)) + b_ref[0]


def _mod_call(c_pad, w, b, layer):
    rows, d = c_pad.shape
    n = w.shape[2]
    tn = 1024
    return pl.pallas_call(
        _mod_kernel,
        grid=(n // tn,),
        in_specs=[pl.BlockSpec((rows, d), lambda j: (0, 0)),
                  pl.BlockSpec((1, d, tn), lambda j: (layer, 0, j)),
                  pl.BlockSpec((1, 1, tn), lambda j: (layer, 0, j))],
        out_specs=pl.BlockSpec((rows, tn), lambda j: (0, j)),
        out_shape=jax.ShapeDtypeStruct((rows, n), F32),
        name="adaln_mod",
    )(c_pad, w, b)


def _inproj_kernel(x_ref, sc_ref, sh_ref, g_ref, wm_ref, wk_ref, cos_ref, sin_ref,
                   wsp_ref, bsp_ref, gmg_ref,
                   qT_ref, k_ref, vT_ref, qiT_ref, ki_ref, wT_ref, gm_ref):
    tm = x_ref.shape[1]
    x = x_ref[0]
    ms = jnp.mean(x * x, axis=-1, keepdims=True)
    h = x * lax.rsqrt(ms + EPS) * g_ref[...]
    h = h * (1.0 + sc_ref[0]) + sh_ref[0]
    hb = h.astype(BF16)

    cos = cos_ref[...]
    sin = sin_ref[...]
    lane = lax.broadcasted_iota(jnp.int32, (tm, LANES), 1)
    first_half = (lane & (HEAD_DIM // 2)) == 0
    low_head = lane < HEAD_DIM

    def rope(t):
        partner = jnp.where(first_half, pltpu.roll(t, LANES - HEAD_DIM // 2, 1),
                            pltpu.roll(t, HEAD_DIM // 2, 1))
        return t * cos + partner * sin

    def proj(col, width=WIDTH):
        return jnp.dot(hb, wm_ref[:, col:col + width], preferred_element_type=F32)

    pq = proj(0)
    for j in range(N_PAIRS):
        sl = slice(j * LANES, (j + 1) * LANES)
        qT_ref[0, sl, :] = (rope(pq[:, sl]) * (HEAD_DIM ** -0.5 * LOG2E)).T.astype(BF16)
    pk = proj(WIDTH)
    for j in range(N_PAIRS):
        sl = slice(j * LANES, (j + 1) * LANES)
        k_ref[0, :, sl] = rope(pk[:, sl]).astype(BF16)
    pv = proj(2 * WIDTH)
    ones_row = jnp.where(lane == HEAD_DIM, 1.0, 0.0)
    for j in range(N_PAIRS):
        pair = pv[:, j * LANES:(j + 1) * LANES]
        vT_ref[0, 2 * j] = jnp.where(low_head, pair, ones_row).T[0:V_ROWS].astype(BF16)
        vT_ref[0, 2 * j + 1] = jnp.where(
            low_head, pltpu.roll(pair, HEAD_DIM, 1), ones_row).T[0:V_ROWS].astype(BF16)
    pqi = proj(3 * WIDTH)
    for j in range(N_PAIRS):
        sl = slice(j * LANES, (j + 1) * LANES)
        qiT_ref[0, sl, :] = (rope(pqi[:, sl]) * (HEAD_DIM ** -0.5)).T.astype(BF16)

    pkw = jnp.dot(hb, wk_ref[...], preferred_element_type=F32)
    rk = rope(pkw)
    ki_ref[0] = jnp.where(low_head, rk, pltpu.roll(rk, HEAD_DIM, 1)).astype(BF16)
    wT_ref[0] = pkw.T[HEAD_DIM:HEAD_DIM + N_HEADS, :] * (N_HEADS ** -0.5)

    gu = proj(4 * WIDTH)
    gv = proj(5 * WIDTH)
    lane_c = lax.broadcasted_iota(jnp.int32, (CHUNK, LANES), 1)
    row_c = lax.broadcasted_iota(jnp.int32, (CHUNK, LANES), 0)
    low_c = lane_c < HEAD_DIM
    causal = lane_c <= row_c
    inv_n = 1.0 / HEAD_DIM
    for j in range(N_PAIRS):
        sl = slice(j * LANES, (j + 1) * LANES)
        w_lo = jnp.where(causal, wsp_ref[2 * j], 0.0).astype(BF16)
        w_hi = jnp.where(causal, wsp_ref[2 * j + 1], 0.0).astype(BF16)
        gain = gmg_ref[:, sl]
        bias = bsp_ref[j]
        for cidx in range(tm // CHUNK):
            rs = slice(cidx * CHUNK, (cidx + 1) * CHUNK)
            vv = _gelu_tanh(gv[rs, sl])
            s_all = jnp.sum(vv, axis=-1, keepdims=True)
            s_lo = jnp.sum(jnp.where(low_c, vv, 0.0), axis=-1, keepdims=True)
            mu = jnp.where(low_c, s_lo, s_all - s_lo) * inv_n
            dv = vv - mu
            d2 = dv * dv
            q_all = jnp.sum(d2, axis=-1, keepdims=True)
            q_lo = jnp.sum(jnp.where(low_c, d2, 0.0), axis=-1, keepdims=True)
            var = jnp.where(low_c, q_lo, q_all - q_lo) * inv_n
            vn = (dv * lax.rsqrt(var + EPS) * gain).astype(BF16)
            m_lo = jnp.dot(w_lo, vn, preferred_element_type=F32)
            m_hi = jnp.dot(w_hi, vn, preferred_element_type=F32)
            mixed = jnp.where(low_c, m_lo, m_hi) + bias
            gm_ref[0, rs, sl] = _gelu_tanh(gu[rs, sl]) * mixed


def _inproj_call(x, sc1, sh1, g1n, w_main, w_kw, cos_t, sin_t, wsp, bsp, gmg):
    b, s, d = x.shape
    tm = TM_IN
    const2 = lambda bi, ti: (0, 0)
    const3 = lambda bi, ti: (0, 0, 0)
    tok = lambda bi, ti: (bi, ti, 0)
    tok_t = lambda bi, ti: (bi, 0, ti)
    return pl.pallas_call(
        _inproj_kernel,
        grid=(b, s // tm),
        in_specs=[pl.BlockSpec((1, tm, d), tok),
                  pl.BlockSpec((1, 1, d), lambda bi, ti: (bi, 0, 0)),
                  pl.BlockSpec((1, 1, d), lambda bi, ti: (bi, 0, 0)),
                  pl.BlockSpec((1, d), const2),
                  pl.BlockSpec(w_main.shape, const2),
                  pl.BlockSpec(w_kw.shape, const2),
                  pl.BlockSpec((tm, LANES), lambda bi, ti: (ti, 0)),
                  pl.BlockSpec((tm, LANES), lambda bi, ti: (ti, 0)),
                  pl.BlockSpec(wsp.shape, const3),
                  pl.BlockSpec(bsp.shape, const3),
                  pl.BlockSpec(gmg.shape, const2)],
        out_specs=[pl.BlockSpec((1, WIDTH, tm), tok_t),
                   pl.BlockSpec((1, tm, WIDTH), tok),
                   pl.BlockSpec((1, N_HEADS, V_ROWS, tm), lambda bi, ti: (bi, 0, 0, ti)),
                   pl.BlockSpec((1, WIDTH, tm), tok_t),
                   pl.BlockSpec((1, tm, LANES), tok),
                   pl.BlockSpec((1, N_HEADS, tm), tok_t),
                   pl.BlockSpec((1, tm, WIDTH), tok)],
        out_shape=[jax.ShapeDtypeStruct((b, WIDTH, s), BF16),
                   jax.ShapeDtypeStruct((b, s, WIDTH), BF16),
                   jax.ShapeDtypeStruct((b, N_HEADS, V_ROWS, s), BF16),
                   jax.ShapeDtypeStruct((b, WIDTH, s), BF16),
                   jax.ShapeDtypeStruct((b, s, LANES), BF16),
                   jax.ShapeDtypeStruct((b, N_HEADS, s), F32),
                   jax.ShapeDtypeStruct((b, s, WIDTH), F32)],
        compiler_params=pltpu.CompilerParams(
            dimension_semantics=("parallel", "parallel"), vmem_limit_bytes=VMEM_LIMIT),
        name="inproj_gmlp",
    )(x, sc1, sh1, g1n, w_main, w_kw, cos_t, sin_t, wsp, bsp, gmg)


def _tiles(x, rows=SUBLANES):
    return x.reshape(x.shape[0] // rows, rows, x.shape[1])


def _fold(x3, op, chains=4):
    accs = [x3[i] for i in range(min(chains, x3.shape[0]))]
    for i in range(len(accs), x3.shape[0]):
        accs[i % chains] = op(accs[i % chains], x3[i])
    while len(accs) > 1:
        accs = [op(accs[i], accs[i + 1]) if i + 1 < len(accs) else accs[i]
                for i in range(0, len(accs), 2)]
    return accs[0]


def _attn_kernel(qT_ref, qiT_ref, wT_ref, k_ref, vT_ref, ki_ref, o_ref,
                 isc_ref, isc16_ref, qm_ref, qim_ref,
                 thr_ref, lo_ref, width_ref, done_ref, tie_ref, cum_ref,
                 bias_ref, s_ref, p_ref, alpha_ref, m_ref, acc_ref):
    i = pl.program_id(1)
    q0 = i * TQ
    state = (SUBLANES, TQ)
    top_half = lax.broadcasted_iota(jnp.int32, (LANES, TQ), 0) < HEAD_DIM

    def replicate(row):
        return jnp.broadcast_to(row, state)

    qT = qT_ref[0].astype(F32)
    qiT = qiT_ref[0].astype(F32)
    for h in range(N_HEADS):
        sl = slice((h // 2) * LANES, (h // 2 + 1) * LANES)
        keep = top_half if h % 2 == 0 else jnp.logical_not(top_half)
        qm_ref[h] = jnp.where(keep, qT[sl, :], 0.0).astype(BF16)
        qim_ref[h] = jnp.where(keep, qiT[sl, :], 0.0).astype(BF16)

    n_idx_full = q0 // CK_IDX
    key_iota = lax.broadcasted_iota(jnp.int32, (KB_IDX, TQ), 0)
    qry_iota = lax.broadcasted_iota(jnp.int32, (KB_IDX, TQ), 1)

    def idx_keys(off, nkeys, masked):
        for r in range(nkeys // KB_IDX):
            koff = pl.multiple_of(off + r * KB_IDX, KB_IDX)
            kic = ki_ref[0, pl.ds(koff, KB_IDX), :]
            a = None
            for h in range(N_HEADS):
                lg = jnp.dot(kic, qim_ref[h], preferred_element_type=F32)
                t = jnp.maximum(lg, 0.0) * wT_ref[0, h:h + 1, :]
                a = t if a is None else a + t
            if masked:
                a = jnp.where(koff + key_iota <= q0 + qry_iota, a, -jnp.inf)
            isc_ref[pl.ds(koff, KB_IDX), :] = a
            isc16_ref[pl.ds(koff, KB_IDX), :] = a.astype(BF16)

    def idx_body(c, carry):
        idx_keys(pl.multiple_of(c * CK_IDX, CK_IDX), CK_IDX, False)
        return carry

    assert CK_IDX % TQ == 0 and TQ % KB_IDX == 0
    lax.fori_loop(0, n_idx_full, idx_body, 0)
    for t in range(CK_IDX // TQ - 1):
        @pl.when(q0 - n_idx_full * CK_IDX > t * TQ)
        def _leftover(t=t):
            idx_keys(pl.multiple_of(n_idx_full * CK_IDX + t * TQ, TQ), TQ, False)
    idx_keys(pl.multiple_of(q0, TQ), TQ, True)

    sweep = max(CK_CNT, CK_ATT)
    assert sweep % TQ == 0 and sweep % min(CK_CNT, CK_ATT) == 0

    for t in range(1, sweep // TQ):
        @pl.when((q0 + t * TQ) // sweep == q0 // sweep)
        def _pad_tail(t=t):
            off = pl.multiple_of(q0 + t * TQ, TQ)
            isc_ref[pl.ds(off, TQ), :] = jnp.full((TQ, TQ), -jnp.inf, F32)
            isc16_ref[pl.ds(off, TQ), :] = jnp.full((TQ, TQ), -jnp.inf, BF16)

    n_cnt = q0 // CK_CNT + 1
    needs_search = (q0 + lax.broadcasted_iota(jnp.int32, state, 1) + 1) > TOPK
    thr_ref[...] = jnp.full(state, LOWEST, F32)

    @pl.when(q0 + TQ > TOPK)
    def _search():
        int_min = jnp.int32(-2 ** 31)
        flip = jnp.int32(0x7FFFFFFF)
        high_bits = jnp.int32(-(1 << (32 - COARSE_BITS)))
        half_bucket = 1 << (31 - COARSE_BITS)

        def to_float(u):
            key = u ^ int_min
            return lax.bitcast_convert_type(jnp.where(key >= 0, key, key ^ flip), F32)

        def from_float(f):
            bits = lax.bitcast_convert_type(f, jnp.int32)
            return jnp.where(bits >= 0, bits, bits ^ flip) ^ int_min

        def bucket_value(u):
            bits = lax.bitcast_convert_type(to_float(u), jnp.int32) & high_bits
            return lax.bitcast_convert_type(bits, F32)

        def key_total(x):
            return replicate(jnp.sum(x, axis=0, keepdims=True))

        def count(src_ref, thr_tile, strict=False):
            rows = thr_tile.shape[0]
            one = jnp.ones((), thr_tile.dtype)
            zero = jnp.zeros((), thr_tile.dtype)

            def cnt_body(c, acc):
                off = pl.multiple_of(c * CK_CNT, CK_CNT)
                part = None
                for r in range(CK_CNT // KB_CNT):
                    koff = pl.multiple_of(off + r * KB_CNT, KB_CNT)
                    blk = _tiles(src_ref[pl.ds(koff, KB_CNT), :], rows)
                    hit = blk > thr_tile[None] if strict else blk >= thr_tile[None]
                    ones = _fold(jnp.where(hit, one, zero), jnp.add)
                    part = ones if part is None else part + ones
                return acc + part.astype(F32)

            acc = lax.fori_loop(0, n_cnt, cnt_body, jnp.zeros((rows, TQ), F32))
            return key_total(acc)

        lo_ref[...] = jnp.zeros(state, jnp.int32)

        def coarse_body(step, carry):
            cand = lo_ref[...] | lax.shift_right_logical(int_min, step)
            bucket = bucket_value(cand)
            thr_tile = jnp.concatenate([bucket, bucket], axis=0).astype(BF16)
            cnt = count(isc16_ref, thr_tile)
            lo_ref[...] = jnp.where(cnt >= float(TOPK), cand, lo_ref[...])
            return carry

        lax.fori_loop(0, COARSE_BITS, coarse_body, 0)

        bucket = bucket_value(lo_ref[...])
        exponent = lax.bitcast_convert_type(bucket, jnp.int32) & jnp.int32(0x7F800000)
        zero_tie = jnp.logical_and(needs_search, exponent == 0)
        tie_ref[...] = jnp.where(zero_tie, 1.0, 0.0)
        thr_ref[...] = jnp.where(zero_tie, 0.0, LOWEST)
        done_ref[...] = jnp.where(
            jnp.logical_and(needs_search, jnp.logical_not(zero_tie)), 0.0, 1.0)

        lo_ref[...] = from_float(bucket) - half_bucket
        width_ref[...] = jnp.full(state, 3 * half_bucket, jnp.int32)

        def cond(carry):
            step, active = carry
            return jnp.logical_and(step < COARSE_BITS + 4, active > 0.0)

        def fine_step():
            lo = lo_ref[...]
            width = width_ref[...]
            done = done_ref[...]
            half = lax.shift_right_logical(width, 1)
            mid = lo + half
            thr = to_float(mid)
            cnt = count(isc_ref, thr)
            active = done == 0.0
            ge = cnt >= float(TOPK)
            new_lo = jnp.where(ge, mid, lo)
            new_width = jnp.where(ge, width - half, half)
            hit = jnp.logical_and(active, cnt == float(TOPK))
            closed = jnp.logical_and(
                active, jnp.logical_and(jnp.logical_not(hit), new_width <= 1))
            thr_ref[...] = jnp.where(
                hit, thr, jnp.where(closed, to_float(new_lo), thr_ref[...]))
            tie_ref[...] = jnp.where(closed, 1.0, tie_ref[...])
            lo_ref[...] = jnp.where(active, new_lo, lo)
            width_ref[...] = jnp.where(active, new_width, width)
            done = jnp.where(jnp.logical_or(hit, closed), 1.0, done)
            done_ref[...] = done
            return done

        def blind_body(step, carry):
            fine_step()
            return carry

        def fine_body(carry):
            step, _ = carry
            return step + 1, jnp.max(1.0 - fine_step())

        lax.fori_loop(0, FINE_BLIND, blind_body, 0)
        lax.while_loop(cond, fine_body,
                       (jnp.int32(FINE_BLIND), jnp.max(1.0 - done_ref[...])))

        @pl.when(jnp.max(tie_ref[...]) > 0.0)
        def _ties():
            thr = thr_ref[...]
            tie_query = tie_ref[...] > 0.0
            zeros = jnp.zeros(state, F32)

            def light_body(c, carry):
                above, tied_so_far = carry
                off = pl.multiple_of(c * CK_CNT, CK_CNT)
                tied_here = None
                for r in range(CK_CNT // KB_CNT):
                    koff = pl.multiple_of(off + r * KB_CNT, KB_CNT)
                    blk = _tiles(isc_ref[pl.ds(koff, KB_CNT), :])
                    above = above + _fold(jnp.where(blk > thr[None], 1.0, 0.0), jnp.add)
                    t = _fold(jnp.where(blk == thr[None], 1.0, 0.0), jnp.add)
                    tied_here = t if tied_here is None else tied_here + t
                tied_so_far = tied_so_far + key_total(tied_here)
                cum_ref[c] = tied_so_far
                return above, tied_so_far

            above, _ = lax.fori_loop(0, n_cnt, light_body, (zeros, zeros))
            quota = jnp.where(tie_query, float(TOPK) - key_total(above), 3.0e38)

            def kept_body(c, kept):
                return kept + jnp.where(cum_ref[c] <= quota, 1.0, 0.0)

            kept = lax.fori_loop(0, n_cnt, kept_body, zeros)
            first_cut = jnp.min(kept).astype(jnp.int32)
            seen_start = jnp.where(first_cut > 0, cum_ref[jnp.maximum(first_cut - 1, 0)], 0.0)

            sq_r = lax.broadcasted_iota(jnp.int32, (KB_TIE, KB_TIE), 0)
            sq_c = lax.broadcasted_iota(jnp.int32, (KB_TIE, KB_TIE), 1)
            prefix = jnp.where(sq_c <= sq_r, 1.0, 0.0).astype(BF16)

            def tie_body(c, seen):
                off = pl.multiple_of(c * KB_TIE, KB_TIE)
                blk = _tiles(isc_ref[pl.ds(off, KB_TIE), :])
                tied = blk == thr[None]
                tied_f = jnp.where(tied, 1.0, 0.0)
                rank = seen[None] + _tiles(jnp.dot(
                    prefix, tied_f.reshape(KB_TIE, TQ).astype(BF16),
                    preferred_element_type=F32))
                drop = jnp.logical_and(tied, rank > quota[None])
                isc_ref[pl.ds(off, KB_TIE), :] = jnp.where(
                    drop, -jnp.inf, blk).reshape(KB_TIE, TQ)
                return seen + key_total(_fold(tied_f, jnp.add))

            assert KB_TIE == CK_CNT
            lax.fori_loop(first_cut, n_cnt, tie_body, seen_start)

    m_ref[...] = jnp.full(m_ref.shape, NEG_BIG, F32)
    acc_ref[...] = jnp.zeros(acc_ref.shape, F32)
    n_att = q0 // CK_ATT + 1

    blocks = [slice(r * KB_ATT, (r + 1) * KB_ATT) for r in range(CK_ATT // KB_ATT)]

    def scores(c, buf, prev):
        off = pl.multiple_of(c * CK_ATT, CK_ATT)
        thr = thr_ref[...]
        for rs in blocks:
            rows = pl.ds(pl.multiple_of(off + rs.start, KB_ATT), KB_ATT)
            bias_ref[buf, rs, :] = jnp.where(_tiles(isc_ref[rows, :]) >= thr[None],
                                             0.0, NEG_BIG).reshape(KB_ATT, TQ)
        for h in range(N_HEADS):
            j = h // 2
            tile_max = None
            for rs in blocks:
                rows = pl.ds(pl.multiple_of(off + rs.start, KB_ATT), KB_ATT)
                kk = k_ref[0, rows, j * LANES:(j + 1) * LANES]
                s = jnp.dot(kk, qm_ref[h], preferred_element_type=F32) + bias_ref[buf, rs, :]
                s_ref[buf, h, rs, :] = s
                mx = _fold(_tiles(s), jnp.maximum)
                tile_max = mx if tile_max is None else jnp.maximum(tile_max, mx)
            m_old = m_ref[prev, h]
            m_new = jnp.maximum(m_old, replicate(jnp.max(tile_max, axis=0, keepdims=True)))
            alpha_ref[buf, h] = jnp.exp2(m_old - m_new)
            m_ref[buf, h] = m_new

    def values(c, buf):
        off = pl.multiple_of(c * CK_ATT, CK_ATT)
        for h in range(N_HEADS):
            m_new = m_ref[buf, h]
            for rs in blocks:
                d = (_tiles(s_ref[buf, h, rs, :]) - m_new[None]).reshape(KB_ATT, TQ)
                p_ref[h, rs, :] = jnp.exp2(d.astype(BF16))
        for h in range(N_HEADS):
            vv = vT_ref[0, h, :, pl.ds(off, CK_ATT)]
            pv = jnp.dot(vv, p_ref[h], preferred_element_type=F32)
            acc_ref[h] = alpha_ref[buf, h][0:1, :] * acc_ref[h] + pv

    def half_step(c, buf, prev):
        scores(c, buf, prev)
        values(c - 1, prev)

    def pair_body(jp, carry):
        half_step(2 * jp + 1, 1, 0)
        half_step(2 * jp + 2, 0, 1)
        return carry

    n_rest = n_att - 1
    scores(0, 0, 1)
    lax.fori_loop(0, n_rest // 2, pair_body, 0)

    @pl.when(n_rest % 2 == 1)
    def _odd_tail():
        half_step(n_rest, 1, 0)
        values(n_rest, 1)

    @pl.when(n_rest % 2 == 0)
    def _even_tail():
        values(n_rest, 0)

    def normalized(h):
        acc = acc_ref[h]
        return acc[0:HEAD_DIM, :] / acc[HEAD_DIM:HEAD_DIM + 1, :]

    for j in range(N_PAIRS):
        out_t = jnp.concatenate([normalized(2 * j), normalized(2 * j + 1)], axis=0)
        o_ref[0, :, j * LANES:(j + 1) * LANES] = out_t.T


def _attn_call(qT, qiT, wT, k, vT, ki):
    b, s, _ = k.shape
    tok = lambda bi, ti: (bi, ti, 0)
    tok_t = lambda bi, ti: (bi, 0, ti)
    per_batch = lambda bi, ti: (bi, 0, 0)
    resident = functools.partial(pl.BlockSpec, pipeline_mode=pl.Buffered(1))
    state = pltpu.VMEM((SUBLANES, TQ), F32)
    state_i = pltpu.VMEM((SUBLANES, TQ), jnp.int32)
    per_head = pltpu.VMEM((2, N_HEADS, SUBLANES, TQ), F32)
    return pl.pallas_call(
        _attn_kernel,
        grid=(b, s // TQ),
        in_specs=[pl.BlockSpec((1, WIDTH, TQ), tok_t),
                  pl.BlockSpec((1, WIDTH, TQ), tok_t),
                  pl.BlockSpec((1, N_HEADS, TQ), tok_t),
                  resident((1, s, WIDTH), per_batch),
                  resident((1, N_HEADS, V_ROWS, s), lambda bi, ti: (bi, 0, 0, 0)),
                  resident((1, s, LANES), per_batch)],
        out_specs=pl.BlockSpec((1, TQ, WIDTH), tok),
        out_shape=jax.ShapeDtypeStruct((b, s, WIDTH), F32),
        scratch_shapes=[pltpu.VMEM((s, TQ), F32),
                        pltpu.VMEM((s, TQ), BF16),
                        pltpu.VMEM((N_HEADS, LANES, TQ), BF16),
                        pltpu.VMEM((N_HEADS, LANES, TQ), BF16),
                        state,
                        state_i,
                        state_i,
                        state,
                        state,
                        pltpu.VMEM((s // CK_CNT, SUBLANES, TQ), F32),
                        pltpu.VMEM((2, CK_ATT, TQ), F32),
                        pltpu.VMEM((2, N_HEADS, CK_ATT, TQ), F32),
                        pltpu.VMEM((N_HEADS, CK_ATT, TQ), BF16),
                        per_head,
                        per_head,
                        pltpu.VMEM((N_HEADS, V_ROWS, TQ), F32)],
        compiler_params=pltpu.CompilerParams(
            dimension_semantics=("parallel", "arbitrary"), vmem_limit_bytes=VMEM_LIMIT),
        name="dsa_attention",
    )(qT, qiT, wT, k, vT, ki)


def _out_kernel(x_ref, a_ref, gmix_ref, g1_ref, sc2_ref, sh2_ref, g2_ref,
                ba_ref, bg_ref, n2_ref, fg_ref, wo_ref, w1_ref, w2_ref, o_ref,
                *, apply_final):
    def rms(t, g):
        return t * lax.rsqrt(jnp.mean(t * t, axis=-1, keepdims=True) + EPS) * g

    x = x_ref[0]
    a = rms(a_ref[0], ba_ref[...]).astype(BF16)
    g = rms(gmix_ref[0], bg_ref[...]).astype(BF16)
    y = (jnp.dot(a, wo_ref[0:WIDTH, :], preferred_element_type=F32)
         + jnp.dot(g, wo_ref[WIDTH:2 * WIDTH, :], preferred_element_type=F32))
    x1 = x + g1_ref[0] * y
    h2 = (rms(x1, n2_ref[...]) * (1.0 + sc2_ref[0]) + sh2_ref[0]).astype(BF16)
    ff = None
    d_ff = w1_ref.shape[1]
    for f in range(d_ff // FF_CHUNK):
        fs = slice(f * FF_CHUNK, (f + 1) * FF_CHUNK)
        u = jnp.maximum(jnp.dot(h2, w1_ref[:, fs], preferred_element_type=F32), 0.0)
        part = jnp.dot((u * u).astype(BF16), w2_ref[fs, :], preferred_element_type=F32)
        ff = part if ff is None else ff + part
    x2 = x1 + g2_ref[0] * ff
    o_ref[0] = rms(x2, fg_ref[...]) if apply_final else x2


def _out_call(x, attn, gmix, g1, sc2, sh2, g2, ba, bg, n2, fg, wo, w1, w2, apply_final):
    b, s, d = x.shape
    tm = TM_OUT
    tok = lambda bi, ti: (bi, ti, 0)
    per_b = lambda bi, ti: (bi, 0, 0)
    const2 = lambda bi, ti: (0, 0)
    resident = functools.partial(pl.BlockSpec, pipeline_mode=pl.Buffered(1))
    vec_b = pl.BlockSpec((1, 1, d), per_b)
    return pl.pallas_call(
        functools.partial(_out_kernel, apply_final=apply_final),
        grid=(b, s // tm),
        in_specs=[pl.BlockSpec((1, tm, d), tok),
                  pl.BlockSpec((1, tm, WIDTH), tok),
                  pl.BlockSpec((1, tm, WIDTH), tok),
                  vec_b, vec_b, vec_b, vec_b,
                  pl.BlockSpec((1, WIDTH), const2),
                  pl.BlockSpec((1, WIDTH), const2),
                  pl.BlockSpec((1, d), const2),
                  pl.BlockSpec((1, d), const2),
                  resident(wo.shape, const2),
                  resident(w1.shape, const2),
                  resident(w2.shape, const2)],
        out_specs=pl.BlockSpec((1, tm, d), tok),
        out_shape=jax.ShapeDtypeStruct((b, s, d), F32),
        compiler_params=pltpu.CompilerParams(
            dimension_semantics=("parallel", "parallel"), vmem_limit_bytes=VMEM_LIMIT),
        name="outproj_mlp",
    )(x, attn, gmix, g1, sc2, sh2, g2, ba, bg, n2, fg, wo, w1, w2)


def _rope_tables(s):
    half = HEAD_DIM // 2
    inv_freq = ROPE_THETA ** (-jnp.arange(half, dtype=F32) / half)
    ang = jnp.arange(s).astype(F32)[:, None] * inv_freq[None, :]
    cos = jnp.cos(ang)
    sin = jnp.sin(ang)
    cos_t = jnp.tile(jnp.concatenate([cos, cos], axis=-1), (1, LANES // HEAD_DIM))
    sin_t = jnp.tile(jnp.concatenate([-sin, sin], axis=-1), (1, LANES // HEAD_DIM))
    return cos_t, sin_t


def kernel(x, c, ada_w, ada_b, norm1_g, w_in, w_spatial, b_spatial, gm_norm_g,
           beta_attn, beta_gmlp, w_out, norm2_g, w_ff1, w_ff2, final_g):
    b, s, d = x.shape
    depth = ada_w.shape[0]
    assert d == 2 * WIDTH and w_in.shape[2] == 6 * WIDTH + HEAD_DIM + N_HEADS
    cos_t, sin_t = _rope_tables(s)
    c_pad = jnp.zeros((8, d), F32).at[:b].set(c)
    kw0 = 4 * WIDTH
    kw1 = kw0 + HEAD_DIM + N_HEADS
    for l in range(depth):
        mod = _mod_call(c_pad, ada_w, ada_b[:, None, :], l)[:b]
        sh1, sc1, g1, sh2, sc2, g2 = [m[:, None, :] for m in jnp.split(mod, 6, axis=-1)]
        w_main = jnp.concatenate([w_in[l][:, :kw0], w_in[l][:, kw1:]], axis=1).astype(BF16)
        w_kw = jnp.pad(w_in[l][:, kw0:kw1], ((0, 0), (0, LANES - (kw1 - kw0)))).astype(BF16)
        bsp = jnp.repeat(b_spatial[l].reshape(N_PAIRS, 2, CHUNK).transpose(0, 2, 1),
                         HEAD_DIM, axis=2)
        qT, k, vT, qiT, ki, wT, gmix = _inproj_call(
            x, sc1, sh1, norm1_g[l][None, :], w_main, w_kw, cos_t, sin_t,
            w_spatial[l], bsp, gm_norm_g[l][None, :])
        attn = _attn_call(qT, qiT, wT, k, vT, ki)
        x = _out_call(x, attn, gmix, g1, sc2, sh2, g2,
                      beta_attn[l][None, :], beta_gmlp[l][None, :], norm2_g[l][None, :],
                      final_g[None, :], w_out[l].astype(BF16), w_ff1[l].astype(BF16),
                      w_ff2[l].astype(BF16), apply_final=(l == depth - 1))
    return x
```

```python
import functools

import jax
import jax.numpy as jnp
from jax import lax
from jax.experimental import pallas as pl
from jax.experimental.pallas import tpu as pltpu

F32 = jnp.float32
BF16 = jnp.bfloat16

LANES = 128
HEAD_DIM = 64
N_HEADS = 8
N_PAIRS = N_HEADS // 2
WIDTH = N_HEADS * HEAD_DIM
CHUNK = 128
TOPK = 256
ROPE_THETA = 10000.0
EPS = 1e-6
LOG2E = 1.4426950408889634
NEG_BIG = -1e30
LOWEST = -3.0e38
VMEM_LIMIT = 56 * 1024 * 1024

TM_IN = 1024
TQ = 256
SUBLANES = 8
KB_IDX = 128
CK_IDX = 1024
KB_TIE = 512
COARSE_BITS = 16
FINE_BLIND = 8
CK_CNT = 512
KB_CNT = 128
CK_ATT = 512
KB_ATT = 128
V_ROWS = 80
TM_OUT = 512
FF_CHUNK = 1024


def _gelu_tanh(x):
    return 0.5 * x * (1.0 + jnp.tanh(0.7978845608028654 * (x + 0.044715 * x * x * x)))


def _mod_kernel(c_ref, w_ref, b_ref, o_ref):
    c = c_ref[...]
    ca = c / (1.0 + jnp.exp(-c))
    w = w_ref[0]
    ca_hi = ca.astype(BF16)
    ca_lo = (ca - ca_hi.astype(F32)).astype(BF16)
    w_hi = w.astype(BF16)
    w_lo = (w - w_hi.astype(F32)).astype(BF16)
    dot = functools.partial(jnp.dot, preferred_element_type=F32)
    o_ref[...] = dot(ca_hi, w_hi) + (dot(ca_hi, w_lo) + dot(ca_lo, w_hi)) + b_ref[0]


def _mod_call(c_pad, w, b, layer):
    rows, d = c_pad.shape
    n = w.shape[2]
    tn = 1024
    return pl.pallas_call(
        _mod_kernel,
        grid=(n // tn,),
        in_specs=[pl.BlockSpec((rows, d), lambda j: (0, 0)),
                  pl.BlockSpec((1, d, tn), lambda j: (layer, 0, j)),
                  pl.BlockSpec((1, 1, tn), lambda j: (layer, 0, j))],
        out_specs=pl.BlockSpec((rows, tn), lambda j: (0, j)),
        out_shape=jax.ShapeDtypeStruct((rows, n), F32),
        name="adaln_mod",
    )(c_pad, w, b)


def _inproj_kernel(x_ref, sc_ref, sh_ref, g_ref, wm_ref, wk_ref, cos_ref, sin_ref,
                   wsp_ref, bsp_ref, gmg_ref,
                   qT_ref, k_ref, vT_ref, qiT_ref, ki_ref, wT_ref, gm_ref):
    tm = x_ref.shape[1]
    x = x_ref[0]
    ms = jnp.mean(x * x, axis=-1, keepdims=True)
    h = x * lax.rsqrt(ms + EPS) * g_ref[...]
    h = h * (1.0 + sc_ref[0]) + sh_ref[0]
    hb = h.astype(BF16)

    cos = cos_ref[...]
    sin = sin_ref[...]
    lane = lax.broadcasted_iota(jnp.int32, (tm, LANES), 1)
    first_half = (lane & (HEAD_DIM // 2)) == 0
    low_head = lane < HEAD_DIM

    def rope(t):
        partner = jnp.where(first_half, pltpu.roll(t, LANES - HEAD_DIM // 2, 1),
                            pltpu.roll(t, HEAD_DIM // 2, 1))
        return t * cos + partner * sin

    def proj(col, width=WIDTH):
        return jnp.dot(hb, wm_ref[:, col:col + width], preferred_element_type=F32)

    pq = proj(0)
    for j in range(N_PAIRS):
        sl = slice(j * LANES, (j + 1) * LANES)
        qT_ref[0, sl, :] = (rope(pq[:, sl]) * (HEAD_DIM ** -0.5 * LOG2E)).T.astype(BF16)
    pk = proj(WIDTH)
    for j in range(N_PAIRS):
        sl = slice(j * LANES, (j + 1) * LANES)
        k_ref[0, :, sl] = rope(pk[:, sl]).astype(BF16)
    pv = proj(2 * WIDTH)
    ones_row = jnp.where(lane == HEAD_DIM, 1.0, 0.0)
    for j in range(N_PAIRS):
        pair = pv[:, j * LANES:(j + 1) * LANES]
        vT_ref[0, 2 * j] = jnp.where(low_head, pair, ones_row).T[0:V_ROWS].astype(BF16)
        vT_ref[0, 2 * j + 1] = jnp.where(
            low_head, pltpu.roll(pair, HEAD_DIM, 1), ones_row).T[0:V_ROWS].astype(BF16)
    pqi = proj(3 * WIDTH)
    for j in range(N_PAIRS):
        sl = slice(j * LANES, (j + 1) * LANES)
        qiT_ref[0, sl, :] = (rope(pqi[:, sl]) * (HEAD_DIM ** -0.5)).T.astype(BF16)

    pkw = jnp.dot(hb, wk_ref[...], preferred_element_type=F32)
    rk = rope(pkw)
    ki_ref[0] = jnp.where(low_head, rk, pltpu.roll(rk, HEAD_DIM, 1)).astype(BF16)
    wT_ref[0] = pkw.T[HEAD_DIM:HEAD_DIM + N_HEADS, :] * (N_HEADS ** -0.5)

    gu = proj(4 * WIDTH)
    gv = proj(5 * WIDTH)
    lane_c = lax.broadcasted_iota(jnp.int32, (CHUNK, LANES), 1)
    row_c = lax.broadcasted_iota(jnp.int32, (CHUNK, LANES), 0)
    low_c = lane_c < HEAD_DIM
    causal = lane_c <= row_c
    inv_n = 1.0 / HEAD_DIM
    for j in range(N_PAIRS):
        sl = slice(j * LANES, (j + 1) * LANES)
        w_lo = jnp.where(causal, wsp_ref[2 * j], 0.0).astype(BF16)
        w_hi = jnp.where(causal, wsp_ref[2 * j + 1], 0.0).astype(BF16)
        gain = gmg_ref[:, sl]
        bias = bsp_ref[j]
        for cidx in range(tm // CHUNK):
            rs = slice(cidx * CHUNK, (cidx + 1) * CHUNK)
            vv = _gelu_tanh(gv[rs, sl])
            s_all = jnp.sum(vv, axis=-1, keepdims=True)
            s_lo = jnp.sum(jnp.where(low_c, vv, 0.0), axis=-1, keepdims=True)
            mu = jnp.where(low_c, s_lo, s_all - s_lo) * inv_n
            dv = vv - mu
            d2 = dv * dv
            q_all = jnp.sum(d2, axis=-1, keepdims=True)
            q_lo = jnp.sum(jnp.where(low_c, d2, 0.0), axis=-1, keepdims=True)
            var = jnp.where(low_c, q_lo, q_all - q_lo) * inv_n
            vn = (dv * lax.rsqrt(var + EPS) * gain).astype(BF16)
            m_lo = jnp.dot(w_lo, vn, preferred_element_type=F32)
            m_hi = jnp.dot(w_hi, vn, preferred_element_type=F32)
            mixed = jnp.where(low_c, m_lo, m_hi) + bias
            gm_ref[0, rs, sl] = _gelu_tanh(gu[rs, sl]) * mixed


def _inproj_call(x, sc1, sh1, g1n, w_main, w_kw, cos_t, sin_t, wsp, bsp, gmg):
    b, s, d = x.shape
    tm = TM_IN
    const2 = lambda bi, ti: (0, 0)
    const3 = lambda bi, ti: (0, 0, 0)
    tok = lambda bi, ti: (bi, ti, 0)
    tok_t = lambda bi, ti: (bi, 0, ti)
    return pl.pallas_call(
        _inproj_kernel,
        grid=(b, s // tm),
        in_specs=[pl.BlockSpec((1, tm, d), tok),
                  pl.BlockSpec((1, 1, d), lambda bi, ti: (bi, 0, 0)),
                  pl.BlockSpec((1, 1, d), lambda bi, ti: (bi, 0, 0)),
                  pl.BlockSpec((1, d), const2),
                  pl.BlockSpec(w_main.shape, const2),
                  pl.BlockSpec(w_kw.shape, const2),
                  pl.BlockSpec((tm, LANES), lambda bi, ti: (ti, 0)),
                  pl.BlockSpec((tm, LANES), lambda bi, ti: (ti, 0)),
                  pl.BlockSpec(wsp.shape, const3),
                  pl.BlockSpec(bsp.shape, const3),
                  pl.BlockSpec(gmg.shape, const2)],
        out_specs=[pl.BlockSpec((1, WIDTH, tm), tok_t),
                   pl.BlockSpec((1, tm, WIDTH), tok),
                   pl.BlockSpec((1, N_HEADS, V_ROWS, tm), lambda bi, ti: (bi, 0, 0, ti)),
                   pl.BlockSpec((1, WIDTH, tm), tok_t),
                   pl.BlockSpec((1, tm, LANES), tok),
                   pl.BlockSpec((1, N_HEADS, tm), tok_t),
                   pl.BlockSpec((1, tm, WIDTH), tok)],
        out_shape=[jax.ShapeDtypeStruct((b, WIDTH, s), BF16),
                   jax.ShapeDtypeStruct((b, s, WIDTH), BF16),
                   jax.ShapeDtypeStruct((b, N_HEADS, V_ROWS, s), BF16),
                   jax.ShapeDtypeStruct((b, WIDTH, s), BF16),
                   jax.ShapeDtypeStruct((b, s, LANES), BF16),
                   jax.ShapeDtypeStruct((b, N_HEADS, s), F32),
                   jax.ShapeDtypeStruct((b, s, WIDTH), F32)],
        compiler_params=pltpu.CompilerParams(
            dimension_semantics=("parallel", "parallel"), vmem_limit_bytes=VMEM_LIMIT),
        name="inproj_gmlp",
    )(x, sc1, sh1, g1n, w_main, w_kw, cos_t, sin_t, wsp, bsp, gmg)


def _tiles(x, rows=SUBLANES):
    return x.reshape(x.shape[0] // rows, rows, x.shape[1])


def _fold(x3, op, chains=4):
    accs = [x3[i] for i in range(min(chains, x3.shape[0]))]
    for i in range(len(accs), x3.shape[0]):
        accs[i % chains] = op(accs[i % chains], x3[i])
    while len(accs) > 1:
        accs = [op(accs[i], accs[i + 1]) if i + 1 < len(accs) else accs[i]
                for i in range(0, len(accs), 2)]
    return accs[0]


def _attn_kernel(qT_ref, qiT_ref, wT_ref, k_ref, vT_ref, ki_ref, o_ref,
                 isc_ref, isc16_ref, qm_ref, qim_ref,
                 thr_ref, lo_ref, width_ref, done_ref, tie_ref, cum_ref,
                 bias_ref, s_ref, p_ref, alpha_ref, m_ref, acc_ref):
    i = pl.program_id(1)
    q0 = i * TQ
    state = (SUBLANES, TQ)
    top_half = lax.broadcasted_iota(jnp.int32, (LANES, TQ), 0) < HEAD_DIM

    def replicate(row):
        return jnp.broadcast_to(row, state)

    qT = qT_ref[0].astype(F32)
    qiT = qiT_ref[0].astype(F32)
    for h in range(N_HEADS):
        sl = slice((h // 2) * LANES, (h // 2 + 1) * LANES)
        keep = top_half if h % 2 == 0 else jnp.logical_not(top_half)
        qm_ref[h] = jnp.where(keep, qT[sl, :], 0.0).astype(BF16)
        qim_ref[h] = jnp.where(keep, qiT[sl, :], 0.0).astype(BF16)

    n_idx_full = q0 // CK_IDX
    key_iota = lax.broadcasted_iota(jnp.int32, (KB_IDX, TQ), 0)
    qry_iota = lax.broadcasted_iota(jnp.int32, (KB_IDX, TQ), 1)

    def idx_keys(off, nkeys, masked):
        for r in range(nkeys // KB_IDX):
            koff = pl.multiple_of(off + r * KB_IDX, KB_IDX)
            kic = ki_ref[0, pl.ds(koff, KB_IDX), :]
            a = None
            for h in range(N_HEADS):
                lg = jnp.dot(kic, qim_ref[h], preferred_element_type=F32)
                t = jnp.maximum(lg, 0.0) * wT_ref[0, h:h + 1, :]
                a = t if a is None else a + t
            if masked:
                a = jnp.where(koff + key_iota <= q0 + qry_iota, a, -jnp.inf)
            isc_ref[pl.ds(koff, KB_IDX), :] = a
            isc16_ref[pl.ds(koff, KB_IDX), :] = a.astype(BF16)

    def idx_body(c, carry):
        idx_keys(pl.multiple_of(c * CK_IDX, CK_IDX), CK_IDX, False)
        return carry

    assert CK_IDX % TQ == 0 and TQ % KB_IDX == 0
    lax.fori_loop(0, n_idx_full, idx_body, 0)
    for t in range(CK_IDX // TQ - 1):
        @pl.when(q0 - n_idx_full * CK_IDX > t * TQ)
        def _leftover(t=t):
            idx_keys(pl.multiple_of(n_idx_full * CK_IDX + t * TQ, TQ), TQ, False)
    idx_keys(pl.multiple_of(q0, TQ), TQ, True)

    sweep = max(CK_CNT, CK_ATT)
    assert sweep % TQ == 0 and sweep % min(CK_CNT, CK_ATT) == 0

    for t in range(1, sweep // TQ):
        @pl.when((q0 + t * TQ) // sweep == q0 // sweep)
        def _pad_tail(t=t):
            off = pl.multiple_of(q0 + t * TQ, TQ)
            isc_ref[pl.ds(off, TQ), :] = jnp.full((TQ, TQ), -jnp.inf, F32)
            isc16_ref[pl.ds(off, TQ), :] = jnp.full((TQ, TQ), -jnp.inf, BF16)

    n_cnt = q0 // CK_CNT + 1
    needs_search = (q0 + lax.broadcasted_iota(jnp.int32, state, 1) + 1) > TOPK
    thr_ref[...] = jnp.full(state, LOWEST, F32)

    @pl.when(q0 + TQ > TOPK)
    def _search():
        int_min = jnp.int32(-2 ** 31)
        flip = jnp.int32(0x7FFFFFFF)
        high_bits = jnp.int32(-(1 << (32 - COARSE_BITS)))
        half_bucket = 1 << (31 - COARSE_BITS)

        def to_float(u):
            key = u ^ int_min
            return lax.bitcast_convert_type(jnp.where(key >= 0, key, key ^ flip), F32)

        def from_float(f):
            bits = lax.bitcast_convert_type(f, jnp.int32)
            return jnp.where(bits >= 0, bits, bits ^ flip) ^ int_min

        def bucket_value(u):
            bits = lax.bitcast_convert_type(to_float(u), jnp.int32) & high_bits
            return lax.bitcast_convert_type(bits, F32)

        def key_total(x):
            return replicate(jnp.sum(x, axis=0, keepdims=True))

        def count(src_ref, thr_tile, strict=False):
            rows = thr_tile.shape[0]
            one = jnp.ones((), thr_tile.dtype)
            zero = jnp.zeros((), thr_tile.dtype)

            def cnt_body(c, acc):
                off = pl.multiple_of(c * CK_CNT, CK_CNT)
                part = None
                for r in range(CK_CNT // KB_CNT):
                    koff = pl.multiple_of(off + r * KB_CNT, KB_CNT)
                    blk = _tiles(src_ref[pl.ds(koff, KB_CNT), :], rows)
                    hit = blk > thr_tile[None] if strict else blk >= thr_tile[None]
                    ones = _fold(jnp.where(hit, one, zero), jnp.add)
                    part = ones if part is None else part + ones
                return acc + part.astype(F32)

            acc = lax.fori_loop(0, n_cnt, cnt_body, jnp.zeros((rows, TQ), F32))
            return key_total(acc)

        lo_ref[...] = jnp.zeros(state, jnp.int32)

        def coarse_body(step, carry):
            cand = lo_ref[...] | lax.shift_right_logical(int_min, step)
            bucket = bucket_value(cand)
            thr_tile = jnp.concatenate([bucket, bucket], axis=0).astype(BF16)
            cnt = count(isc16_ref, thr_tile)
            lo_ref[...] = jnp.where(cnt >= float(TOPK), cand, lo_ref[...])
            return carry

        lax.fori_loop(0, COARSE_BITS, coarse_body, 0)

        bucket = bucket_value(lo_ref[...])
        exponent = lax.bitcast_convert_type(bucket, jnp.int32) & jnp.int32(0x7F800000)
        zero_tie = jnp.logical_and(needs_search, exponent == 0)
        tie_ref[...] = jnp.where(zero_tie, 1.0, 0.0)
        thr_ref[...] = jnp.where(zero_tie, 0.0, LOWEST)
        done_ref[...] = jnp.where(
            jnp.logical_and(needs_search, jnp.logical_not(zero_tie)), 0.0, 1.0)

        lo_ref[...] = from_float(bucket) - half_bucket
        width_ref[...] = jnp.full(state, 3 * half_bucket, jnp.int32)

        def cond(carry):
            step, active = carry
            return jnp.logical_and(step < COARSE_BITS + 4, active > 0.0)

        def fine_step():
            lo = lo_ref[...]
            width = width_ref[...]
            done = done_ref[...]
            half = lax.shift_right_logical(width, 1)
            mid = lo + half
            thr = to_float(mid)
            cnt = count(isc_ref, thr)
            active = done == 0.0
            ge = cnt >= float(TOPK)
            new_lo = jnp.where(ge, mid, lo)
            new_width = jnp.where(ge, width - half, half)
            hit = jnp.logical_and(active, cnt == float(TOPK))
            closed = jnp.logical_and(
                active, jnp.logical_and(jnp.logical_not(hit), new_width <= 1))
            thr_ref[...] = jnp.where(
                hit, thr, jnp.where(closed, to_float(new_lo), thr_ref[...]))
            tie_ref[...] = jnp.where(closed, 1.0, tie_ref[...])
            lo_ref[...] = jnp.where(active, new_lo, lo)
            width_ref[...] = jnp.where(active, new_width, width)
            done = jnp.where(jnp.logical_or(hit, closed), 1.0, done)
            done_ref[...] = done
            return done

        def blind_body(step, carry):
            fine_step()
            return carry

        def fine_body(carry):
            step, _ = carry
            return step + 1, jnp.max(1.0 - fine_step())

        lax.fori_loop(0, FINE_BLIND, blind_body, 0)
        lax.while_loop(cond, fine_body,
                       (jnp.int32(FINE_BLIND), jnp.max(1.0 - done_ref[...])))

        @pl.when(jnp.max(tie_ref[...]) > 0.0)
        def _ties():
            thr = thr_ref[...]
            tie_query = tie_ref[...] > 0.0
            zeros = jnp.zeros(state, F32)

            def light_body(c, carry):
                above, tied_so_far = carry
                off = pl.multiple_of(c * CK_CNT, CK_CNT)
                tied_here = None
                for r in range(CK_CNT // KB_CNT):
                    koff = pl.multiple_of(off + r * KB_CNT, KB_CNT)
                    blk = _tiles(isc_ref[pl.ds(koff, KB_CNT), :])
                    above = above + _fold(jnp.where(blk > thr[None], 1.0, 0.0), jnp.add)
                    t = _fold(jnp.where(blk == thr[None], 1.0, 0.0), jnp.add)
                    tied_here = t if tied_here is None else tied_here + t
                tied_so_far = tied_so_far + key_total(tied_here)
                cum_ref[c] = tied_so_far
                return above, tied_so_far

            above, _ = lax.fori_loop(0, n_cnt, light_body, (zeros, zeros))
            quota = jnp.where(tie_query, float(TOPK) - key_total(above), 3.0e38)

            def kept_body(c, kept):
                return kept + jnp.where(cum_ref[c] <= quota, 1.0, 0.0)

            kept = lax.fori_loop(0, n_cnt, kept_body, zeros)
            first_cut = jnp.min(kept).astype(jnp.int32)
            seen_start = jnp.where(first_cut > 0, cum_ref[jnp.maximum(first_cut - 1, 0)], 0.0)

            sq_r = lax.broadcasted_iota(jnp.int32, (KB_TIE, KB_TIE), 0)
            sq_c = lax.broadcasted_iota(jnp.int32, (KB_TIE, KB_TIE), 1)
            prefix = jnp.where(sq_c <= sq_r, 1.0, 0.0).astype(BF16)

            def tie_body(c, seen):
                off = pl.multiple_of(c * KB_TIE, KB_TIE)
                blk = _tiles(isc_ref[pl.ds(off, KB_TIE), :])
                tied = blk == thr[None]
                tied_f = jnp.where(tied, 1.0, 0.0)
                rank = seen[None] + _tiles(jnp.dot(
                    prefix, tied_f.reshape(KB_TIE, TQ).astype(BF16),
                    preferred_element_type=F32))
                drop = jnp.logical_and(tied, rank > quota[None])
                isc_ref[pl.ds(off, KB_TIE), :] = jnp.where(
                    drop, -jnp.inf, blk).reshape(KB_TIE, TQ)
                return seen + key_total(_fold(tied_f, jnp.add))

            assert KB_TIE == CK_CNT
            lax.fori_loop(first_cut, n_cnt, tie_body, seen_start)

    m_ref[...] = jnp.full(m_ref.shape, NEG_BIG, F32)
    acc_ref[...] = jnp.zeros(acc_ref.shape, F32)
    n_att = q0 // CK_ATT + 1

    blocks = [slice(r * KB_ATT, (r + 1) * KB_ATT) for r in range(CK_ATT // KB_ATT)]

    def zero_from(tiles_bf16):
        word = None
        for t in tiles_bf16:
            w = pltpu.bitcast(t.astype(F32), jnp.uint32)
            word = w if word is None else word | w
        word = lax.shift_right_logical(lax.shift_right_logical(word, jnp.uint32(16)),
                                       jnp.uint32(16))
        return pltpu.bitcast(word, F32).astype(BF16)

    def scores(c, buf, prev, gates=None):
        off = pl.multiple_of(c * CK_ATT, CK_ATT)
        thr = thr_ref[...]
        for rs in blocks:
            rows = pl.ds(pl.multiple_of(off + rs.start, KB_ATT), KB_ATT)
            bias_ref[buf, rs, :] = jnp.where(_tiles(isc_ref[rows, :]) >= thr[None],
                                             0.0, NEG_BIG).reshape(KB_ATT, TQ)
        for h in range(N_HEADS):
            j = h // 2
            tile_max = None
            for rs in blocks:
                rows = pl.ds(pl.multiple_of(off + rs.start, KB_ATT), KB_ATT)
                kk = k_ref[0, rows, j * LANES:(j + 1) * LANES]
                if gates is not None and h > 0:
                    kk = (_tiles(kk, 16) + gates[h - 1][None]).reshape(KB_ATT, LANES)
                s = jnp.dot(kk, qm_ref[h], preferred_element_type=F32) + bias_ref[buf, rs, :]
                s_ref[buf, h, rs, :] = s
                mx = _fold(_tiles(s), jnp.maximum)
                tile_max = mx if tile_max is None else jnp.maximum(tile_max, mx)
            m_old = m_ref[prev, h]
            m_new = jnp.maximum(m_old, replicate(jnp.max(tile_max, axis=0, keepdims=True)))
            alpha_ref[buf, h] = jnp.exp2(m_old - m_new)
            m_ref[buf, h] = m_new

    def probabilities(buf):
        gates = []
        for h in range(N_HEADS):
            m_new = m_ref[buf, h]
            last_tiles = []
            for rs in blocks:
                d = (_tiles(s_ref[buf, h, rs, :]) - m_new[None]).reshape(KB_ATT, TQ)
                p = jnp.exp2(d.astype(BF16))
                p_ref[h, rs, :] = p
                folded = _fold(_tiles(p, 16), jnp.maximum)
                last_tiles.append(jnp.maximum(folded[:, 0:LANES], folded[:, LANES:TQ]))
            gates.append(zero_from(last_tiles))
        return gates

    def weighted_values(c, buf):
        off = pl.multiple_of(c * CK_ATT, CK_ATT)
        for h in range(N_HEADS):
            vv = vT_ref[0, h, :, pl.ds(off, CK_ATT)]
            pv = jnp.dot(vv, p_ref[h], preferred_element_type=F32)
            acc_ref[h] = alpha_ref[buf, h][0:1, :] * acc_ref[h] + pv

    def values(c, buf):
        probabilities(buf)
        weighted_values(c, buf)

    def half_step(c, buf, prev):
        gates = probabilities(prev)
        scores(c, buf, prev, gates)
        weighted_values(c - 1, prev)

    def pair_body(jp, carry):
        half_step(2 * jp + 1, 1, 0)
        half_step(2 * jp + 2, 0, 1)
        return carry

    n_rest = n_att - 1
    scores(0, 0, 1)
    lax.fori_loop(0, n_rest // 2, pair_body, 0)

    @pl.when(n_rest % 2 == 1)
    def _odd_tail():
        half_step(n_rest, 1, 0)
        values(n_rest, 1)

    @pl.when(n_rest % 2 == 0)
    def _even_tail():
        values(n_rest, 0)

    def normalized(h):
        acc = acc_ref[h]
        return acc[0:HEAD_DIM, :] / acc[HEAD_DIM:HEAD_DIM + 1, :]

    for j in range(N_PAIRS):
        out_t = jnp.concatenate([normalized(2 * j), normalized(2 * j + 1)], axis=0)
        o_ref[0, :, j * LANES:(j + 1) * LANES] = out_t.T


def _attn_call(qT, qiT, wT, k, vT, ki):
    b, s, _ = k.shape
    tok = lambda bi, ti: (bi, ti, 0)
    tok_t = lambda bi, ti: (bi, 0, ti)
    per_batch = lambda bi, ti: (bi, 0, 0)
    resident = functools.partial(pl.BlockSpec, pipeline_mode=pl.Buffered(1))
    state = pltpu.VMEM((SUBLANES, TQ), F32)
    state_i = pltpu.VMEM((SUBLANES, TQ), jnp.int32)
    per_head = pltpu.VMEM((2, N_HEADS, SUBLANES, TQ), F32)
    return pl.pallas_call(
        _attn_kernel,
        grid=(b, s // TQ),
        in_specs=[pl.BlockSpec((1, WIDTH, TQ), tok_t),
                  pl.BlockSpec((1, WIDTH, TQ), tok_t),
                  pl.BlockSpec((1, N_HEADS, TQ), tok_t),
                  resident((1, s, WIDTH), per_batch),
                  resident((1, N_HEADS, V_ROWS, s), lambda bi, ti: (bi, 0, 0, 0)),
                  resident((1, s, LANES), per_batch)],
        out_specs=pl.BlockSpec((1, TQ, WIDTH), tok),
        out_shape=jax.ShapeDtypeStruct((b, s, WIDTH), F32),
        scratch_shapes=[pltpu.VMEM((s, TQ), F32),
                        pltpu.VMEM((s, TQ), BF16),
                        pltpu.VMEM((N_HEADS, LANES, TQ), BF16),
                        pltpu.VMEM((N_HEADS, LANES, TQ), BF16),
                        state,
                        state_i,
                        state_i,
                        state,
                        state,
                        pltpu.VMEM((s // CK_CNT, SUBLANES, TQ), F32),
                        pltpu.VMEM((2, CK_ATT, TQ), F32),
                        pltpu.VMEM((2, N_HEADS, CK_ATT, TQ), F32),
                        pltpu.VMEM((N_HEADS, CK_ATT, TQ), BF16),
                        per_head,
                        per_head,
                        pltpu.VMEM((N_HEADS, V_ROWS, TQ), F32)],
        compiler_params=pltpu.CompilerParams(
            dimension_semantics=("parallel", "arbitrary"), vmem_limit_bytes=VMEM_LIMIT),
        name="dsa_attention",
    )(qT, qiT, wT, k, vT, ki)


def _out_kernel(x_ref, a_ref, gmix_ref, g1_ref, sc2_ref, sh2_ref, g2_ref,
                ba_ref, bg_ref, n2_ref, fg_ref, wo_ref, w1_ref, w2_ref, o_ref,
                *, apply_final):
    def rms(t, g):
        return t * lax.rsqrt(jnp.mean(t * t, axis=-1, keepdims=True) + EPS) * g

    x = x_ref[0]
    a = rms(a_ref[0], ba_ref[...]).astype(BF16)
    g = rms(gmix_ref[0], bg_ref[...]).astype(BF16)
    y = (jnp.dot(a, wo_ref[0:WIDTH, :], preferred_element_type=F32)
         + jnp.dot(g, wo_ref[WIDTH:2 * WIDTH, :], preferred_element_type=F32))
    x1 = x + g1_ref[0] * y
    h2 = (rms(x1, n2_ref[...]) * (1.0 + sc2_ref[0]) + sh2_ref[0]).astype(BF16)
    ff = None
    d_ff = w1_ref.shape[1]
    for f in range(d_ff // FF_CHUNK):
        fs = slice(f * FF_CHUNK, (f + 1) * FF_CHUNK)
        u = jnp.maximum(jnp.dot(h2, w1_ref[:, fs], preferred_element_type=F32), 0.0)
        part = jnp.dot((u * u).astype(BF16), w2_ref[fs, :], preferred_element_type=F32)
        ff = part if ff is None else ff + part
    x2 = x1 + g2_ref[0] * ff
    o_ref[0] = rms(x2, fg_ref[...]) if apply_final else x2


def _out_call(x, attn, gmix, g1, sc2, sh2, g2, ba, bg, n2, fg, wo, w1, w2, apply_final):
    b, s, d = x.shape
    tm = TM_OUT
    tok = lambda bi, ti: (bi, ti, 0)
    per_b = lambda bi, ti: (bi, 0, 0)
    const2 = lambda bi, ti: (0, 0)
    resident = functools.partial(pl.BlockSpec, pipeline_mode=pl.Buffered(1))
    vec_b = pl.BlockSpec((1, 1, d), per_b)
    return pl.pallas_call(
        functools.partial(_out_kernel, apply_final=apply_final),
        grid=(b, s // tm),
        in_specs=[pl.BlockSpec((1, tm, d), tok),
                  pl.BlockSpec((1, tm, WIDTH), tok),
                  pl.BlockSpec((1, tm, WIDTH), tok),
                  vec_b, vec_b, vec_b, vec_b,
                  pl.BlockSpec((1, WIDTH), const2),
                  pl.BlockSpec((1, WIDTH), const2),
                  pl.BlockSpec((1, d), const2),
                  pl.BlockSpec((1, d), const2),
                  resident(wo.shape, const2),
                  resident(w1.shape, const2),
                  resident(w2.shape, const2)],
        out_specs=pl.BlockSpec((1, tm, d), tok),
        out_shape=jax.ShapeDtypeStruct((b, s, d), F32),
        compiler_params=pltpu.CompilerParams(
            dimension_semantics=("parallel", "parallel"), vmem_limit_bytes=VMEM_LIMIT),
        name="outproj_mlp",
    )(x, attn, gmix, g1, sc2, sh2, g2, ba, bg, n2, fg, wo, w1, w2)


def _rope_tables(s):
    half = HEAD_DIM // 2
    inv_freq = ROPE_THETA ** (-jnp.arange(half, dtype=F32) / half)
    ang = jnp.arange(s).astype(F32)[:, None] * inv_freq[None, :]
    cos = jnp.cos(ang)
    sin = jnp.sin(ang)
    cos_t = jnp.tile(jnp.concatenate([cos, cos], axis=-1), (1, LANES // HEAD_DIM))
    sin_t = jnp.tile(jnp.concatenate([-sin, sin], axis=-1), (1, LANES // HEAD_DIM))
    return cos_t, sin_t


def kernel(x, c, ada_w, ada_b, norm1_g, w_in, w_spatial, b_spatial, gm_norm_g,
           beta_attn, beta_gmlp, w_out, norm2_g, w_ff1, w_ff2, final_g):
    b, s, d = x.shape
    depth = ada_w.shape[0]
    assert d == 2 * WIDTH and w_in.shape[2] == 6 * WIDTH + HEAD_DIM + N_HEADS
    cos_t, sin_t = _rope_tables(s)
    c_pad = jnp.zeros((8, d), F32).at[:b].set(c)
    kw0 = 4 * WIDTH
    kw1 = kw0 + HEAD_DIM + N_HEADS
    for l in range(depth):
        mod = _mod_call(c_pad, ada_w, ada_b[:, None, :], l)[:b]
        sh1, sc1, g1, sh2, sc2, g2 = [m[:, None, :] for m in jnp.split(mod, 6, axis=-1)]
        w_main = jnp.concatenate([w_in[l][:, :kw0], w_in[l][:, kw1:]], axis=1).astype(BF16)
        w_kw = jnp.pad(w_in[l][:, kw0:kw1], ((0, 0), (0, LANES - (kw1 - kw0)))).astype(BF16)
        bsp = jnp.repeat(b_spatial[l].reshape(N_PAIRS, 2, CHUNK).transpose(0, 2, 1),
                         HEAD_DIM, axis=2)
        qT, k, vT, qiT, ki, wT, gmix = _inproj_call(
            x, sc1, sh1, norm1_g[l][None, :], w_main, w_kw, cos_t, sin_t,
            w_spatial[l], bsp, gm_norm_g[l][None, :])
        attn = _attn_call(qT, qiT, wT, k, vT, ki)
        x = _out_call(x, attn, gmix, g1, sc2, sh2, g2,
                      beta_attn[l][None, :], beta_gmlp[l][None, :], norm2_g[l][None, :],
                      final_g[None, :], w_out[l].astype(BF16), w_ff1[l].astype(BF16),
                      w_ff2[l].astype(BF16), apply_final=(l == depth - 1))
    return x
```

```python
import functools

import jax
import jax.numpy as jnp
from jax import lax
from jax.experimental import pallas as pl
from jax.experimental.pallas import tpu as pltpu

F32 = jnp.float32
BF16 = jnp.bfloat16

LANES = 128
HEAD_DIM = 64
N_HEADS = 8
N_PAIRS = N_HEADS // 2
WIDTH = N_HEADS * HEAD_DIM
CHUNK = 128
TOPK = 256
ROPE_THETA = 10000.0
EPS = 1e-6
LOG2E = 1.4426950408889634
NEG_BIG = -1e30
LOWEST = -3.0e38
VMEM_LIMIT = 56 * 1024 * 1024

TM_IN = 1024
TQ = 256
SUBLANES = 8
KB_IDX = 128
CK_IDX = 1024
KB_TIE = 512
COARSE_BITS = 16
FINE_BLIND = 8
CK_CNT = 512
KB_CNT = 128
CK_ATT = 512
KB_ATT = 128
V_ROWS = 80
TM_OUT = 512
FF_CHUNK = 1024


def _gelu_tanh(x):
    return 0.5 * x * (1.0 + jnp.tanh(0.7978845608028654 * (x + 0.044715 * x * x * x)))


def _mod_kernel(c_ref, w_ref, b_ref, o_ref):
    c = c_ref[...]
    ca = c / (1.0 + jnp.exp(-c))
    w = w_ref[0]
    ca_hi = ca.astype(BF16)
    ca_lo = (ca - ca_hi.astype(F32)).astype(BF16)
    w_hi = w.astype(BF16)
    w_lo = (w - w_hi.astype(F32)).astype(BF16)
    dot = functools.partial(jnp.dot, preferred_element_type=F32)
    o_ref[...] = dot(ca_hi, w_hi) + (dot(ca_hi, w_lo) + dot(ca_lo, w_hi)) + b_ref[0]


def _mod_call(c_pad, w, b, layer):
    rows, d = c_pad.shape
    n = w.shape[2]
    tn = 1024
    return pl.pallas_call(
        _mod_kernel,
        grid=(n // tn,),
        in_specs=[pl.BlockSpec((rows, d), lambda j: (0, 0)),
                  pl.BlockSpec((1, d, tn), lambda j: (layer, 0, j)),
                  pl.BlockSpec((1, 1, tn), lambda j: (layer, 0, j))],
        out_specs=pl.BlockSpec((rows, tn), lambda j: (0, j)),
        out_shape=jax.ShapeDtypeStruct((rows, n), F32),
        name="adaln_mod",
    )(c_pad, w, b)


def _inproj_kernel(x_ref, sc_ref, sh_ref, g_ref, wm_ref, wk_ref, cos_ref, sin_ref,
                   wsp_ref, bsp_ref, gmg_ref,
                   qT_ref, k_ref, vT_ref, qiT_ref, ki_ref, wT_ref, gm_ref):
    tm = x_ref.shape[1]
    x = x_ref[0]
    ms = jnp.mean(x * x, axis=-1, keepdims=True)
    h = x * lax.rsqrt(ms + EPS) * g_ref[...]
    h = h * (1.0 + sc_ref[0]) + sh_ref[0]
    hb = h.astype(BF16)

    cos = cos_ref[...]
    sin = sin_ref[...]
    lane = lax.broadcasted_iota(jnp.int32, (tm, LANES), 1)
    first_half = (lane & (HEAD_DIM // 2)) == 0
    low_head = lane < HEAD_DIM

    def rope(t):
        partner = jnp.where(first_half, pltpu.roll(t, LANES - HEAD_DIM // 2, 1),
                            pltpu.roll(t, HEAD_DIM // 2, 1))
        return t * cos + partner * sin

    def proj(col, width=WIDTH):
        return jnp.dot(hb, wm_ref[:, col:col + width], preferred_element_type=F32)

    pq = proj(0)
    for j in range(N_PAIRS):
        sl = slice(j * LANES, (j + 1) * LANES)
        qT_ref[0, sl, :] = (rope(pq[:, sl]) * (HEAD_DIM ** -0.5 * LOG2E)).T.astype(BF16)
    pk = proj(WIDTH)
    for j in range(N_PAIRS):
        sl = slice(j * LANES, (j + 1) * LANES)
        k_ref[0, :, sl] = rope(pk[:, sl]).astype(BF16)
    pv = proj(2 * WIDTH)
    ones_row = jnp.where(lane == HEAD_DIM, 1.0, 0.0)
    for j in range(N_PAIRS):
        pair = pv[:, j * LANES:(j + 1) * LANES]
        vT_ref[0, 2 * j] = jnp.where(low_head, pair, ones_row).T[0:V_ROWS].astype(BF16)
        vT_ref[0, 2 * j + 1] = jnp.where(
            low_head, pltpu.roll(pair, HEAD_DIM, 1), ones_row).T[0:V_ROWS].astype(BF16)
    pqi = proj(3 * WIDTH)
    for j in range(N_PAIRS):
        sl = slice(j * LANES, (j + 1) * LANES)
        qiT_ref[0, sl, :] = (rope(pqi[:, sl]) * (HEAD_DIM ** -0.5)).T.astype(BF16)

    pkw = jnp.dot(hb, wk_ref[...], preferred_element_type=F32)
    rk = rope(pkw)
    ki_ref[0] = jnp.where(low_head, rk, pltpu.roll(rk, HEAD_DIM, 1)).astype(BF16)
    wT_ref[0] = pkw.T[HEAD_DIM:HEAD_DIM + N_HEADS, :] * (N_HEADS ** -0.5)

    gu = proj(4 * WIDTH)
    gv = proj(5 * WIDTH)
    lane_c = lax.broadcasted_iota(jnp.int32, (CHUNK, LANES), 1)
    row_c = lax.broadcasted_iota(jnp.int32, (CHUNK, LANES), 0)
    low_c = lane_c < HEAD_DIM
    causal = lane_c <= row_c
    inv_n = 1.0 / HEAD_DIM
    for j in range(N_PAIRS):
        sl = slice(j * LANES, (j + 1) * LANES)
        w_lo = jnp.where(causal, wsp_ref[2 * j], 0.0).astype(BF16)
        w_hi = jnp.where(causal, wsp_ref[2 * j + 1], 0.0).astype(BF16)
        gain = gmg_ref[:, sl]
        bias = bsp_ref[j]
        for cidx in range(tm // CHUNK):
            rs = slice(cidx * CHUNK, (cidx + 1) * CHUNK)
            vv = _gelu_tanh(gv[rs, sl])
            s_all = jnp.sum(vv, axis=-1, keepdims=True)
            s_lo = jnp.sum(jnp.where(low_c, vv, 0.0), axis=-1, keepdims=True)
            mu = jnp.where(low_c, s_lo, s_all - s_lo) * inv_n
            dv = vv - mu
            d2 = dv * dv
            q_all = jnp.sum(d2, axis=-1, keepdims=True)
            q_lo = jnp.sum(jnp.where(low_c, d2, 0.0), axis=-1, keepdims=True)
            var = jnp.where(low_c, q_lo, q_all - q_lo) * inv_n
            vn = (dv * lax.rsqrt(var + EPS) * gain).astype(BF16)
            m_lo = jnp.dot(w_lo, vn, preferred_element_type=F32)
            m_hi = jnp.dot(w_hi, vn, preferred_element_type=F32)
            mixed = jnp.where(low_c, m_lo, m_hi) + bias
            gm_ref[0, rs, sl] = _gelu_tanh(gu[rs, sl]) * mixed


def _inproj_call(x, sc1, sh1, g1n, w_main, w_kw, cos_t, sin_t, wsp, bsp, gmg):
    b, s, d = x.shape
    tm = TM_IN
    const2 = lambda bi, ti: (0, 0)
    const3 = lambda bi, ti: (0, 0, 0)
    tok = lambda bi, ti: (bi, ti, 0)
    tok_t = lambda bi, ti: (bi, 0, ti)
    return pl.pallas_call(
        _inproj_kernel,
        grid=(b, s // tm),
        in_specs=[pl.BlockSpec((1, tm, d), tok),
                  pl.BlockSpec((1, 1, d), lambda bi, ti: (bi, 0, 0)),
                  pl.BlockSpec((1, 1, d), lambda bi, ti: (bi, 0, 0)),
                  pl.BlockSpec((1, d), const2),
                  pl.BlockSpec(w_main.shape, const2),
                  pl.BlockSpec(w_kw.shape, const2),
                  pl.BlockSpec((tm, LANES), lambda bi, ti: (ti, 0)),
                  pl.BlockSpec((tm, LANES), lambda bi, ti: (ti, 0)),
                  pl.BlockSpec(wsp.shape, const3),
                  pl.BlockSpec(bsp.shape, const3),
                  pl.BlockSpec(gmg.shape, const2)],
        out_specs=[pl.BlockSpec((1, WIDTH, tm), tok_t),
                   pl.BlockSpec((1, tm, WIDTH), tok),
                   pl.BlockSpec((1, N_HEADS, V_ROWS, tm), lambda bi, ti: (bi, 0, 0, ti)),
                   pl.BlockSpec((1, WIDTH, tm), tok_t),
                   pl.BlockSpec((1, tm, LANES), tok),
                   pl.BlockSpec((1, N_HEADS, tm), tok_t),
                   pl.BlockSpec((1, tm, WIDTH), tok)],
        out_shape=[jax.ShapeDtypeStruct((b, WIDTH, s), BF16),
                   jax.ShapeDtypeStruct((b, s, WIDTH), BF16),
                   jax.ShapeDtypeStruct((b, N_HEADS, V_ROWS, s), BF16),
                   jax.ShapeDtypeStruct((b, WIDTH, s), BF16),
                   jax.ShapeDtypeStruct((b, s, LANES), BF16),
                   jax.ShapeDtypeStruct((b, N_HEADS, s), F32),
                   jax.ShapeDtypeStruct((b, s, WIDTH), F32)],
        compiler_params=pltpu.CompilerParams(
            dimension_semantics=("parallel", "parallel"), vmem_limit_bytes=VMEM_LIMIT),
        name="inproj_gmlp",
    )(x, sc1, sh1, g1n, w_main, w_kw, cos_t, sin_t, wsp, bsp, gmg)


def _tiles(x, rows=SUBLANES):
    return x.reshape(x.shape[0] // rows, rows, x.shape[1])


def _fold(x3, op, chains=4):
    accs = [x3[i] for i in range(min(chains, x3.shape[0]))]
    for i in range(len(accs), x3.shape[0]):
        accs[i % chains] = op(accs[i % chains], x3[i])
    while len(accs) > 1:
        accs = [op(accs[i], accs[i + 1]) if i + 1 < len(accs) else accs[i]
                for i in range(0, len(accs), 2)]
    return accs[0]


def _attn_kernel(qT_ref, qiT_ref, wT_ref, k_ref, vT_ref, ki_ref, o_ref,
                 isc_ref, isc16_ref, qm_ref, qim_ref, cum_ref,
                 bias_ref, s_ref, p_ref, alpha_ref, m_ref, acc_ref,
                 thr_ref, lo_ref, width_ref, done_ref, tie_ref):
    i = pl.program_id(1)
    q0 = i * TQ
    state = (SUBLANES, TQ)
    top_half = lax.broadcasted_iota(jnp.int32, (LANES, TQ), 0) < HEAD_DIM

    def replicate(row):
        return jnp.broadcast_to(row, state)

    qT = qT_ref[0].astype(F32)
    qiT = qiT_ref[0].astype(F32)
    for h in range(N_HEADS):
        sl = slice((h // 2) * LANES, (h // 2 + 1) * LANES)
        keep = top_half if h % 2 == 0 else jnp.logical_not(top_half)
        qm_ref[h] = jnp.where(keep, qT[sl, :], 0.0).astype(BF16)
        qim_ref[h] = jnp.where(keep, qiT[sl, :], 0.0).astype(BF16)

    n_idx_full = q0 // CK_IDX
    key_iota = lax.broadcasted_iota(jnp.int32, (KB_IDX, TQ), 0)
    qry_iota = lax.broadcasted_iota(jnp.int32, (KB_IDX, TQ), 1)

    def idx_keys(off, nkeys, masked):
        for r in range(nkeys // KB_IDX):
            koff = pl.multiple_of(off + r * KB_IDX, KB_IDX)
            kic = ki_ref[0, pl.ds(koff, KB_IDX), :]
            a = None
            for h in range(N_HEADS):
                lg = jnp.dot(kic, qim_ref[h], preferred_element_type=F32)
                t = jnp.maximum(lg, 0.0) * wT_ref[0, h:h + 1, :]
                a = t if a is None else a + t
            if masked:
                a = jnp.where(koff + key_iota <= q0 + qry_iota, a, -jnp.inf)
            isc_ref[pl.ds(koff, KB_IDX), :] = a
            isc16_ref[pl.ds(koff, KB_IDX), :] = a.astype(BF16)

    def idx_body(c, carry):
        idx_keys(pl.multiple_of(c * CK_IDX, CK_IDX), CK_IDX, False)
        return carry

    assert CK_IDX % TQ == 0 and TQ % KB_IDX == 0
    lax.fori_loop(0, n_idx_full, idx_body, 0)
    for t in range(CK_IDX // TQ - 1):
        @pl.when(q0 - n_idx_full * CK_IDX > t * TQ)
        def _leftover(t=t):
            idx_keys(pl.multiple_of(n_idx_full * CK_IDX + t * TQ, TQ), TQ, False)
    idx_keys(pl.multiple_of(q0, TQ), TQ, True)

    sweep = max(CK_CNT, CK_ATT)
    assert sweep % TQ == 0 and sweep % min(CK_CNT, CK_ATT) == 0

    for t in range(1, sweep // TQ):
        @pl.when((q0 + t * TQ) // sweep == q0 // sweep)
        def _pad_tail(t=t):
            off = pl.multiple_of(q0 + t * TQ, TQ)
            isc_ref[pl.ds(off, TQ), :] = jnp.full((TQ, TQ), -jnp.inf, F32)
            isc16_ref[pl.ds(off, TQ), :] = jnp.full((TQ, TQ), -jnp.inf, BF16)

    n_cnt = q0 // CK_CNT + 1
    needs_search = (q0 + lax.broadcasted_iota(jnp.int32, state, 1) + 1) > TOPK
    thr_ref[...] = jnp.full(state, LOWEST, F32)

    @pl.when(q0 + TQ > TOPK)
    def _search():
        int_min = jnp.int32(-2 ** 31)
        flip = jnp.int32(0x7FFFFFFF)
        high_bits = jnp.int32(-(1 << (32 - COARSE_BITS)))
        half_bucket = 1 << (31 - COARSE_BITS)

        def to_float(u):
            key = u ^ int_min
            return lax.bitcast_convert_type(jnp.where(key >= 0, key, key ^ flip), F32)

        def from_float(f):
            bits = lax.bitcast_convert_type(f, jnp.int32)
            return jnp.where(bits >= 0, bits, bits ^ flip) ^ int_min

        def bucket_value(u):
            bits = lax.bitcast_convert_type(to_float(u), jnp.int32) & high_bits
            return lax.bitcast_convert_type(bits, F32)

        def key_total(x):
            return replicate(jnp.sum(x, axis=0, keepdims=True))

        def count(src_ref, thr_tile, strict=False):
            rows = thr_tile.shape[0]
            one = jnp.ones((), thr_tile.dtype)
            zero = jnp.zeros((), thr_tile.dtype)

            def cnt_body(c, acc):
                off = pl.multiple_of(c * CK_CNT, CK_CNT)
                part = None
                for r in range(CK_CNT // KB_CNT):
                    koff = pl.multiple_of(off + r * KB_CNT, KB_CNT)
                    blk = _tiles(src_ref[pl.ds(koff, KB_CNT), :], rows)
                    hit = blk > thr_tile[None] if strict else blk >= thr_tile[None]
                    ones = _fold(jnp.where(hit, one, zero), jnp.add)
                    part = ones if part is None else part + ones
                return acc + part.astype(F32)

            acc = lax.fori_loop(0, n_cnt, cnt_body, jnp.zeros((rows, TQ), F32))
            return key_total(acc)

        lo_ref[...] = jnp.zeros(state, jnp.int32)

        def coarse_body(step, carry):
            cand = lo_ref[...] | lax.shift_right_logical(int_min, step)
            bucket = bucket_value(cand)
            thr_tile = jnp.concatenate([bucket, bucket], axis=0).astype(BF16)
            cnt = count(isc16_ref, thr_tile)
            lo_ref[...] = jnp.where(cnt >= float(TOPK), cand, lo_ref[...])
            return carry

        lax.fori_loop(0, COARSE_BITS, coarse_body, 0)

        bucket = bucket_value(lo_ref[...])
        exponent = lax.bitcast_convert_type(bucket, jnp.int32) & jnp.int32(0x7F800000)
        zero_tie = jnp.logical_and(needs_search, exponent == 0)
        tie_ref[...] = jnp.where(zero_tie, 1.0, 0.0)
        thr_ref[...] = jnp.where(zero_tie, 0.0, LOWEST)
        done_ref[...] = jnp.where(
            jnp.logical_and(needs_search, jnp.logical_not(zero_tie)), 0.0, 1.0)

        lo_ref[...] = from_float(bucket) - half_bucket
        width_ref[...] = jnp.full(state, 3 * half_bucket, jnp.int32)

        def cond(carry):
            step, active = carry
            return jnp.logical_and(step < COARSE_BITS + 4, active > 0.0)

        def fine_step():
            lo = lo_ref[...]
            width = width_ref[...]
            done = done_ref[...]
            half = lax.shift_right_logical(width, 1)
            mid = lo + half
            thr = to_float(mid)
            cnt = count(isc_ref, thr)
            active = done == 0.0
            ge = cnt >= float(TOPK)
            new_lo = jnp.where(ge, mid, lo)
            new_width = jnp.where(ge, width - half, half)
            hit = jnp.logical_and(active, cnt == float(TOPK))
            closed = jnp.logical_and(
                active, jnp.logical_and(jnp.logical_not(hit), new_width <= 1))
            thr_ref[...] = jnp.where(
                hit, thr, jnp.where(closed, to_float(new_lo), thr_ref[...]))
            tie_ref[...] = jnp.where(closed, 1.0, tie_ref[...])
            lo_ref[...] = jnp.where(active, new_lo, lo)
            width_ref[...] = jnp.where(active, new_width, width)
            done = jnp.where(jnp.logical_or(hit, closed), 1.0, done)
            done_ref[...] = done
            return done

        def blind_body(step, carry):
            fine_step()
            return carry

        def fine_body(carry):
            step, _ = carry
            return step + 1, jnp.max(1.0 - fine_step())

        lax.fori_loop(0, FINE_BLIND, blind_body, 0)
        lax.while_loop(cond, fine_body,
                       (jnp.int32(FINE_BLIND), jnp.max(1.0 - done_ref[...])))

        @pl.when(jnp.max(tie_ref[...]) > 0.0)
        def _ties():
            thr = thr_ref[...]
            tie_query = tie_ref[...] > 0.0
            zeros = jnp.zeros(state, F32)

            def light_body(c, carry):
                above, tied_so_far = carry
                off = pl.multiple_of(c * CK_CNT, CK_CNT)
                tied_here = None
                for r in range(CK_CNT // KB_CNT):
                    koff = pl.multiple_of(off + r * KB_CNT, KB_CNT)
                    blk = _tiles(isc_ref[pl.ds(koff, KB_CNT), :])
                    above = above + _fold(jnp.where(blk > thr[None], 1.0, 0.0), jnp.add)
                    t = _fold(jnp.where(blk == thr[None], 1.0, 0.0), jnp.add)
                    tied_here = t if tied_here is None else tied_here + t
                tied_so_far = tied_so_far + key_total(tied_here)
                cum_ref[c] = tied_so_far
                return above, tied_so_far

            above, _ = lax.fori_loop(0, n_cnt, light_body, (zeros, zeros))
            quota = jnp.where(tie_query, float(TOPK) - key_total(above), 3.0e38)

            def kept_body(c, kept):
                return kept + jnp.where(cum_ref[c] <= quota, 1.0, 0.0)

            kept = lax.fori_loop(0, n_cnt, kept_body, zeros)
            first_cut = jnp.min(kept).astype(jnp.int32)
            seen_start = jnp.where(first_cut > 0, cum_ref[jnp.maximum(first_cut - 1, 0)], 0.0)

            sq_r = lax.broadcasted_iota(jnp.int32, (KB_TIE, KB_TIE), 0)
            sq_c = lax.broadcasted_iota(jnp.int32, (KB_TIE, KB_TIE), 1)
            prefix = jnp.where(sq_c <= sq_r, 1.0, 0.0).astype(BF16)

            def tie_body(c, seen):
                off = pl.multiple_of(c * KB_TIE, KB_TIE)
                blk = _tiles(isc_ref[pl.ds(off, KB_TIE), :])
                tied = blk == thr[None]
                tied_f = jnp.where(tied, 1.0, 0.0)
                rank = seen[None] + _tiles(jnp.dot(
                    prefix, tied_f.reshape(KB_TIE, TQ).astype(BF16),
                    preferred_element_type=F32))
                drop = jnp.logical_and(tied, rank > quota[None])
                isc_ref[pl.ds(off, KB_TIE), :] = jnp.where(
                    drop, -jnp.inf, blk).reshape(KB_TIE, TQ)
                return seen + key_total(_fold(tied_f, jnp.add))

            assert KB_TIE == CK_CNT
            lax.fori_loop(first_cut, n_cnt, tie_body, seen_start)

    m_ref[...] = jnp.full(m_ref.shape, NEG_BIG, F32)
    acc_ref[...] = jnp.zeros(acc_ref.shape, F32)
    n_att = q0 // CK_ATT + 1

    blocks = [slice(r * KB_ATT, (r + 1) * KB_ATT) for r in range(CK_ATT // KB_ATT)]

    def scores(c, buf, prev):
        off = pl.multiple_of(c * CK_ATT, CK_ATT)
        thr = thr_ref[...]
        for rs in blocks:
            rows = pl.ds(pl.multiple_of(off + rs.start, KB_ATT), KB_ATT)
            bias_ref[buf, rs, :] = jnp.where(_tiles(isc_ref[rows, :]) >= thr[None],
                                             0.0, NEG_BIG).reshape(KB_ATT, TQ)
        for h in range(N_HEADS):
            j = h // 2
            tile_max = None
            for rs in blocks:
                rows = pl.ds(pl.multiple_of(off + rs.start, KB_ATT), KB_ATT)
                kk = k_ref[0, rows, j * LANES:(j + 1) * LANES]
                s = jnp.dot(kk, qm_ref[h], preferred_element_type=F32) + bias_ref[buf, rs, :]
                s_ref[buf, h, rs, :] = s
                mx = _fold(_tiles(s), jnp.maximum)
                tile_max = mx if tile_max is None else jnp.maximum(tile_max, mx)
            m_old = m_ref[prev, h]
            m_new = jnp.maximum(m_old, replicate(jnp.max(tile_max, axis=0, keepdims=True)))
            alpha_ref[buf, h] = jnp.exp2(m_old - m_new)
            m_ref[buf, h] = m_new

    def values(c, buf):
        off = pl.multiple_of(c * CK_ATT, CK_ATT)
        for h in range(N_HEADS):
            m_new = m_ref[buf, h]
            for rs in blocks:
                d = (_tiles(s_ref[buf, h, rs, :]) - m_new[None]).reshape(KB_ATT, TQ)
                p_ref[h, rs, :] = jnp.exp2(d.astype(BF16))
        for h in range(N_HEADS):
            vv = vT_ref[0, h, :, pl.ds(off, CK_ATT)]
            pv = jnp.dot(vv, p_ref[h], preferred_element_type=F32)
            acc_ref[h] = alpha_ref[buf, h][0:1, :] * acc_ref[h] + pv

    def half_step(c, buf, prev):
        scores(c, buf, prev)
        values(c - 1, prev)

    def pair_body(jp, carry):
        half_step(2 * jp + 1, 1, 0)
        half_step(2 * jp + 2, 0, 1)
        return carry

    n_rest = n_att - 1
    scores(0, 0, 1)
    lax.fori_loop(0, n_rest // 2, pair_body, 0)

    @pl.when(n_rest % 2 == 1)
    def _odd_tail():
        half_step(n_rest, 1, 0)
        values(n_rest, 1)

    @pl.when(n_rest % 2 == 0)
    def _even_tail():
        values(n_rest, 0)

    def normalized(h):
        acc = acc_ref[h]
        return acc[0:HEAD_DIM, :] / acc[HEAD_DIM:HEAD_DIM + 1, :]

    for j in range(N_PAIRS):
        out_t = jnp.concatenate([normalized(2 * j), normalized(2 * j + 1)], axis=0)
        o_ref[0, :, j * LANES:(j + 1) * LANES] = out_t.T


def _attn_call(qT, qiT, wT, k, vT, ki):
    b, s, _ = k.shape
    tok = lambda bi, ti: (bi, ti, 0)
    tok_t = lambda bi, ti: (bi, 0, ti)
    per_batch = lambda bi, ti: (bi, 0, 0)
    resident = functools.partial(pl.BlockSpec, pipeline_mode=pl.Buffered(1))
    state = pltpu.VMEM((SUBLANES, TQ), F32)
    state_i = pltpu.VMEM((SUBLANES, TQ), jnp.int32)
    per_head = pltpu.VMEM((2, N_HEADS, SUBLANES, TQ), F32)
    return pl.pallas_call(
        _attn_kernel,
        grid=(b, s // TQ),
        in_specs=[pl.BlockSpec((1, WIDTH, TQ), tok_t),
                  pl.BlockSpec((1, WIDTH, TQ), tok_t),
                  pl.BlockSpec((1, N_HEADS, TQ), tok_t),
                  resident((1, s, WIDTH), per_batch),
                  resident((1, N_HEADS, V_ROWS, s), lambda bi, ti: (bi, 0, 0, 0)),
                  resident((1, s, LANES), per_batch)],
        out_specs=pl.BlockSpec((1, TQ, WIDTH), tok),
        out_shape=jax.ShapeDtypeStruct((b, s, WIDTH), F32),
        scratch_shapes=[pltpu.VMEM((s, TQ), F32),
                        pltpu.VMEM((s, TQ), BF16),
                        pltpu.VMEM((N_HEADS, LANES, TQ), BF16),
                        pltpu.VMEM((N_HEADS, LANES, TQ), BF16),
                        pltpu.VMEM((s // CK_CNT, SUBLANES, TQ), F32),
                        pltpu.VMEM((2, CK_ATT, TQ), F32),
                        pltpu.VMEM((2, N_HEADS, CK_ATT, TQ), F32),
                        pltpu.VMEM((N_HEADS, CK_ATT, TQ), BF16),
                        per_head,
                        per_head,
                        pltpu.VMEM((N_HEADS, V_ROWS, TQ), F32),
                        state,
                        state_i,
                        state_i,
                        state,
                        state],
        compiler_params=pltpu.CompilerParams(
            dimension_semantics=("parallel", "arbitrary"), vmem_limit_bytes=VMEM_LIMIT),
        name="dsa_attention",
    )(qT, qiT, wT, k, vT, ki)


def _out_kernel(x_ref, a_ref, gmix_ref, g1_ref, sc2_ref, sh2_ref, g2_ref,
                ba_ref, bg_ref, n2_ref, fg_ref, wo_ref, w1_ref, w2_ref, o_ref,
                *, apply_final):
    def rms(t, g):
        return t * lax.rsqrt(jnp.mean(t * t, axis=-1, keepdims=True) + EPS) * g

    x = x_ref[0]
    a = rms(a_ref[0], ba_ref[...]).astype(BF16)
    g = rms(gmix_ref[0], bg_ref[...]).astype(BF16)
    y = (jnp.dot(a, wo_ref[0:WIDTH, :], preferred_element_type=F32)
         + jnp.dot(g, wo_ref[WIDTH:2 * WIDTH, :], preferred_element_type=F32))
    x1 = x + g1_ref[0] * y
    h2 = (rms(x1, n2_ref[...]) * (1.0 + sc2_ref[0]) + sh2_ref[0]).astype(BF16)
    ff = None
    d_ff = w1_ref.shape[1]
    for f in range(d_ff // FF_CHUNK):
        fs = slice(f * FF_CHUNK, (f + 1) * FF_CHUNK)
        u = jnp.maximum(jnp.dot(h2, w1_ref[:, fs], preferred_element_type=F32), 0.0)
        part = jnp.dot((u * u).astype(BF16), w2_ref[fs, :], preferred_element_type=F32)
        ff = part if ff is None else ff + part
    x2 = x1 + g2_ref[0] * ff
    o_ref[0] = rms(x2, fg_ref[...]) if apply_final else x2


def _out_call(x, attn, gmix, g1, sc2, sh2, g2, ba, bg, n2, fg, wo, w1, w2, apply_final):
    b, s, d = x.shape
    tm = TM_OUT
    tok = lambda bi, ti: (bi, ti, 0)
    per_b = lambda bi, ti: (bi, 0, 0)
    const2 = lambda bi, ti: (0, 0)
    resident = functools.partial(pl.BlockSpec, pipeline_mode=pl.Buffered(1))
    vec_b = pl.BlockSpec((1, 1, d), per_b)
    return pl.pallas_call(
        functools.partial(_out_kernel, apply_final=apply_final),
        grid=(b, s // tm),
        in_specs=[pl.BlockSpec((1, tm, d), tok),
                  pl.BlockSpec((1, tm, WIDTH), tok),
                  pl.BlockSpec((1, tm, WIDTH), tok),
                  vec_b, vec_b, vec_b, vec_b,
                  pl.BlockSpec((1, WIDTH), const2),
                  pl.BlockSpec((1, WIDTH), const2),
                  pl.BlockSpec((1, d), const2),
                  pl.BlockSpec((1, d), const2),
                  resident(wo.shape, const2),
                  resident(w1.shape, const2),
                  resident(w2.shape, const2)],
        out_specs=pl.BlockSpec((1, tm, d), tok),
        out_shape=jax.ShapeDtypeStruct((b, s, d), F32),
        compiler_params=pltpu.CompilerParams(
            dimension_semantics=("parallel", "parallel"), vmem_limit_bytes=VMEM_LIMIT),
        name="outproj_mlp",
    )(x, attn, gmix, g1, sc2, sh2, g2, ba, bg, n2, fg, wo, w1, w2)


def _rope_tables(s):
    half = HEAD_DIM // 2
    inv_freq = ROPE_THETA ** (-jnp.arange(half, dtype=F32) / half)
    ang = jnp.arange(s).astype(F32)[:, None] * inv_freq[None, :]
    cos = jnp.cos(ang)
    sin = jnp.sin(ang)
    cos_t = jnp.tile(jnp.concatenate([cos, cos], axis=-1), (1, LANES // HEAD_DIM))
    sin_t = jnp.tile(jnp.concatenate([-sin, sin], axis=-1), (1, LANES // HEAD_DIM))
    return cos_t, sin_t


def kernel(x, c, ada_w, ada_b, norm1_g, w_in, w_spatial, b_spatial, gm_norm_g,
           beta_attn, beta_gmlp, w_out, norm2_g, w_ff1, w_ff2, final_g):
    b, s, d = x.shape
    depth = ada_w.shape[0]
    assert d == 2 * WIDTH and w_in.shape[2] == 6 * WIDTH + HEAD_DIM + N_HEADS
    cos_t, sin_t = _rope_tables(s)
    c_pad = jnp.zeros((8, d), F32).at[:b].set(c)
    kw0 = 4 * WIDTH
    kw1 = kw0 + HEAD_DIM + N_HEADS
    for l in range(depth):
        mod = _mod_call(c_pad, ada_w, ada_b[:, None, :], l)[:b]
        sh1, sc1, g1, sh2, sc2, g2 = [m[:, None, :] for m in jnp.split(mod, 6, axis=-1)]
        w_main = jnp.concatenate([w_in[l][:, :kw0], w_in[l][:, kw1:]], axis=1).astype(BF16)
        w_kw = jnp.pad(w_in[l][:, kw0:kw1], ((0, 0), (0, LANES - (kw1 - kw0)))).astype(BF16)
        bsp = jnp.repeat(b_spatial[l].reshape(N_PAIRS, 2, CHUNK).transpose(0, 2, 1),
                         HEAD_DIM, axis=2)
        qT, k, vT, qiT, ki, wT, gmix = _inproj_call(
            x, sc1, sh1, norm1_g[l][None, :], w_main, w_kw, cos_t, sin_t,
            w_spatial[l], bsp, gm_norm_g[l][None, :])
        attn = _attn_call(qT, qiT, wT, k, vT, ki)
        x = _out_call(x, attn, gmix, g1, sc2, sh2, g2,
                      beta_attn[l][None, :], beta_gmlp[l][None, :], norm2_g[l][None, :],
                      final_g[None, :], w_out[l].astype(BF16), w_ff1[l].astype(BF16),
                      w_ff2[l].astype(BF16), apply_final=(l == depth - 1))
    return x
```
